```python
import jax, jax.numpy as jnp
from jax import lax
import numpy as np

D_MODEL = 2048
BATCH = 8
SEQ = 2048
DEPTH = 2
DEC_BATCH = 128
DEC_SEQ = 4
PAST_LEN = 16384
PAGE_SIZE = 128

D_CONV = D_MODEL // 2
CONV_WIDTH = 31
CONV_STATE = CONV_WIDTH - 1
MLA_HEADS = 8
MLA_NOPE = 128
MLA_ROPE = 64
MLA_V = 128
Q_LORA = 512
KV_LORA = 512
ROPE_THETA = 10000.0
MLA_SCALE = (MLA_NOPE + MLA_ROPE) ** -0.5
D_IN0 = 2 * D_CONV + Q_LORA + KV_LORA + MLA_ROPE
D_MIX0 = D_CONV + MLA_HEADS * MLA_V
SB_HEADS = 16
SB_KV_HEADS = 2
SB_GROUP = SB_HEADS // SB_KV_HEADS
SB_HEAD_DIM = 128
SB_SCALE = SB_HEAD_DIM ** -0.5
D_IN1 = (SB_HEADS + 2 * SB_KV_HEADS) * SB_HEAD_DIM
D_MIX1 = SB_HEADS * SB_HEAD_DIM
D_FF = 5632
N_EXPERTS = 8
TOP_K = 2
D_FF_EXPERT = 7168
Q_BLOCK = 128
EPS = 1e-6

kernel_name = "hybrid_conv_mla_stickbreak_step"

F32 = jnp.float32


def rmsnorm(x, g):
    xf = x.astype(F32)
    y = xf * lax.rsqrt(jnp.mean(xf * xf, axis=-1, keepdims=True) + EPS)
    return (y * g.astype(F32)).astype(x.dtype)


def layernorm(x, g, b):
    xf = x.astype(F32)
    mu = jnp.mean(xf, axis=-1, keepdims=True)
    xc = xf - mu
    y = xc * lax.rsqrt(jnp.mean(xc * xc, axis=-1, keepdims=True) + EPS)
    return (y * g.astype(F32) + b.astype(F32)).astype(x.dtype)


def rope(x, pos):
    half = MLA_ROPE // 2
    freqs = ROPE_THETA ** (-jnp.arange(half, dtype=F32) / half)
    ang = pos.astype(F32)[:, None] * freqs[None, :]
    shape = (ang.shape[0],) + (1,) * (x.ndim - 3) + (half,)
    cos = jnp.cos(ang).reshape(shape)
    sin = jnp.sin(ang).reshape(shape)
    xf = x.astype(F32)
    x1, x2 = xf[..., :half], xf[..., half:]
    return jnp.concatenate([x1 * cos - x2 * sin, x1 * sin + x2 * cos], axis=-1).astype(x.dtype)


def swiglu(x, w1, w3, w2):
    return (jax.nn.silu(x @ w1) * (x @ w3)) @ w2


def moe_swiglu(x, w_router, w1_e, w3_e, w2_e):
    B, T, D = x.shape
    xt = x.reshape(B * T, D)
    logits = jnp.dot(xt.astype(F32), w_router.astype(F32))
    top_vals, top_idx = lax.top_k(logits, TOP_K)
    top_w = jax.nn.softmax(top_vals, axis=-1)
    gate = jnp.sum(jax.nn.one_hot(top_idx, N_EXPERTS, dtype=F32) * top_w[..., None], axis=1)
    y = jnp.zeros_like(xt)
    for e in range(N_EXPERTS):
        y = y + gate[:, e:e + 1].astype(x.dtype) * swiglu(xt, w1_e[e], w3_e[e], w2_e[e])
    return y.reshape(B, T, D)


def conv_module(a, gate, u_past, conv_w, conv_b, ln_g, ln_b):
    u = a * jax.nn.sigmoid(gate)
    u_ext = jnp.concatenate([u_past.astype(u.dtype), u], axis=1)
    h = lax.conv_general_dilated(u_ext, conv_w[:, None, :].astype(u.dtype), window_strides=(1,), padding='VALID',
                                 dimension_numbers=('NWC', 'WIO', 'NWC'), feature_group_count=D_CONV)
    h = jax.nn.silu(layernorm(h + conv_b, ln_g, ln_b))
    return h, u_ext[:, -CONV_STATE:]


def mla_project(c_q, c_kv_raw, k_rope_raw, pos, q_norm_g, kv_norm_g, w_uq, w_uk):
    q = jnp.einsum('btc,chd->bthd', rmsnorm(c_q, q_norm_g), w_uq)
    q_nope, q_rope = q[..., :MLA_NOPE], q[..., MLA_NOPE:]
    q_rope = rope(q_rope, pos)
    q_lat = jnp.einsum('bthn,chn->bthc', q_nope, w_uk)
    c_kv = rmsnorm(c_kv_raw, kv_norm_g)
    k_rope = rope(k_rope_raw, pos)
    return q_lat, q_rope, c_kv, k_rope


def mla_prompt_core(q_lat, q_rope, c_kv, k_rope):
    B, T, H, C = q_lat.shape
    nb = T // Q_BLOCK
    ql_b = jnp.moveaxis(q_lat.reshape(B, nb, Q_BLOCK, H, C), 1, 0)
    qr_b = jnp.moveaxis(q_rope.reshape(B, nb, Q_BLOCK, H, MLA_ROPE), 1, 0)
    ckv = c_kv.astype(F32)
    kr = k_rope.astype(F32)
    kpos = jnp.arange(T)

    def block(args):
        i, ql, qr = args
        s = (jnp.einsum('bthc,bsc->bhts', ql.astype(F32), ckv)
             + jnp.einsum('bthr,bsr->bhts', qr.astype(F32), kr)) * MLA_SCALE
        qpos = i * Q_BLOCK + jnp.arange(Q_BLOCK)
        s = jnp.where(kpos[None, :] <= qpos[:, None], s, -jnp.inf)
        p = jax.nn.softmax(s, axis=-1)
        return jnp.einsum('bhts,bsc->bthc', p, ckv)

    o = lax.map(block, (jnp.arange(nb), ql_b, qr_b))
    return jnp.moveaxis(o, 0, 1).reshape(B, T, H, C)


def mla_sample_core(q_lat, q_rope, c_kv, k_rope, cache_ckv, cache_krope, page_table):
    T = q_lat.shape[1]
    ql = q_lat.astype(F32)
    qr = q_rope.astype(F32)

    def scores(ckv, kr):
        return (jnp.einsum('bthc,bsc->bhts', ql, ckv) + jnp.einsum('bthr,bsr->bhts', qr, kr)) * MLA_SCALE

    ckv_new = c_kv.astype(F32)
    s = jnp.where(jnp.tril(jnp.ones((T, T), bool)), scores(ckv_new, k_rope.astype(F32)), -jnp.inf)
    m = jnp.max(s, axis=-1)
    p = jnp.exp(s - m[..., None])
    l = jnp.sum(p, axis=-1)
    acc = jnp.einsum('bhts,bsc->bhtc', p, ckv_new)

    def page_step(carry, pages):
        m, l, acc = carry
        ckv = cache_ckv[pages].astype(F32)
        kr = cache_krope[pages].astype(F32)
        s = scores(ckv, kr)
        m_new = jnp.maximum(m, jnp.max(s, axis=-1))
        corr = jnp.exp(m - m_new)
        p = jnp.exp(s - m_new[..., None])
        return (m_new, l * corr + jnp.sum(p, axis=-1),
                acc * corr[..., None] + jnp.einsum('bhtp,bpc->bhtc', p, ckv)), None

    (m, l, acc), _ = lax.scan(page_step, (m, l, acc), page_table.T)
    return jnp.transpose(acc / l[..., None], (0, 2, 1, 3))


def stick_break(z, mask, tail):
    log_beta = jax.nn.log_sigmoid(z)
    log_1m = jax.nn.log_sigmoid(-z)
    if mask is not None:
        log_1m = jnp.where(mask, log_1m, 0.0)
    later = lax.cumsum(log_1m, axis=z.ndim - 1, reverse=True) - log_1m
    if tail is not None:
        later = later + tail[..., None]
    w = jnp.exp(log_beta + later)
    if mask is not None:
        w = jnp.where(mask, w, 0.0)
    return w, jnp.sum(log_1m, axis=-1)


def sb_prompt_core(q, k, v):
    B, T = q.shape[:2]
    nb = T // Q_BLOCK
    q_b = jnp.moveaxis(q.reshape(B, nb, Q_BLOCK, SB_KV_HEADS, SB_GROUP, SB_HEAD_DIM), 1, 0)
    kf = k.astype(F32)
    vf = v.astype(F32)
    kpos = jnp.arange(T)

    def block(args):
        i, qb = args
        z = jnp.einsum('btgrd,bsgd->bgrts', qb.astype(F32), kf) * SB_SCALE
        qpos = i * Q_BLOCK + jnp.arange(Q_BLOCK)
        w, _ = stick_break(z, kpos[None, :] < qpos[:, None], None)
        return jnp.einsum('bgrts,bsgd->btgrd', w, vf)

    o = lax.map(block, (jnp.arange(nb), q_b))
    return jnp.moveaxis(o, 0, 1).reshape(B, T, D_MIX1)


def sb_sample_core(q, k, v, cache_k, cache_v, page_table):
    B, T = q.shape[:2]
    qf = q.astype(F32)
    z = jnp.einsum('btgrd,bsgd->bgrts', qf, k.astype(F32)) * SB_SCALE
    w, tail = stick_break(z, jnp.tril(jnp.ones((T, T), bool), k=-1), None)
    acc = jnp.einsum('bgrts,bsgd->bgrtd', w, v.astype(F32))

    def page_step(carry, pages):
        tail, acc = carry
        kp = cache_k[pages].astype(F32)
        vp = cache_v[pages].astype(F32)
        z = jnp.einsum('btgrd,bpgd->bgrtp', qf, kp) * SB_SCALE
        w, page_sum = stick_break(z, None, tail)
        return (tail + page_sum, acc + jnp.einsum('bgrtp,bpgd->bgrtd', w, vp)), None

    (_, acc), _ = lax.scan(page_step, (tail, acc), page_table.T, reverse=True)
    return jnp.transpose(acc, (0, 3, 1, 2, 4)).reshape(B, T, D_MIX1)


def even_layer(x, pos, conv_past, mla_core, ln_mix0, w_in0, conv_w, conv_b, conv_ln_g, conv_ln_b,
               q_norm_g, kv_norm_g, w_uq, w_uk, w_uv, w_out0, ln_ffn0, w1_d, w3_d, w2_d):
    B, T, _ = x.shape
    h = rmsnorm(x, ln_mix0) @ w_in0
    a, gate, c_q, c_kv_raw, k_rope_raw = jnp.split(
        h, [D_CONV, 2 * D_CONV, 2 * D_CONV + Q_LORA, 2 * D_CONV + Q_LORA + KV_LORA], axis=-1)
    conv_out, conv_state = conv_module(a, gate, conv_past, conv_w, conv_b, conv_ln_g, conv_ln_b)
    q_lat, q_rope, c_kv, k_rope = mla_project(c_q, c_kv_raw, k_rope_raw, pos, q_norm_g, kv_norm_g, w_uq, w_uk)
    o_lat = mla_core(q_lat, q_rope, c_kv, k_rope)
    mla_out = jnp.einsum('bthc,chv->bthv', o_lat.astype(x.dtype), w_uv).reshape(B, T, MLA_HEADS * MLA_V)
    x = x + jnp.concatenate([conv_out, mla_out], axis=-1) @ w_out0
    x = x + swiglu(rmsnorm(x, ln_ffn0), w1_d, w3_d, w2_d)
    return x, conv_state, c_kv, k_rope


def odd_layer(x, sb_core, ln_mix1, w_in1, w_out1, ln_ffn1, w_router, w1_e, w3_e, w2_e):
    B, T, _ = x.shape
    h = rmsnorm(x, ln_mix1) @ w_in1
    q, k, v = jnp.split(h, [D_MIX1, D_MIX1 + SB_KV_HEADS * SB_HEAD_DIM], axis=-1)
    q = q.reshape(B, T, SB_KV_HEADS, SB_GROUP, SB_HEAD_DIM)
    k = k.reshape(B, T, SB_KV_HEADS, SB_HEAD_DIM)
    v = v.reshape(B, T, SB_KV_HEADS, SB_HEAD_DIM)
    o = sb_core(q, k, v)
    x = x + o.astype(x.dtype) @ w_out1
    x = x + moe_swiglu(rmsnorm(x, ln_ffn1), w_router, w1_e, w3_e, w2_e)
    return x, k, v


def setup_inputs(seed: int = 0) -> dict:
    key = jax.random.key(seed)
    ks = jax.random.split(key, 40)
    n_pages = PAST_LEN // PAGE_SIZE
    n_used = DEC_BATCH * n_pages
    n_pool = n_used + n_used // 4

    def w(k, shape, fan_in):
        return jax.random.normal(k, shape, F32) * (fan_in ** -0.5)

    def gain(k, n):
        return 1.0 + 0.05 * jax.random.normal(k, (n,), F32)

    def bias(k, n):
        return 0.01 * jax.random.normal(k, (n,), F32)

    page_table = jax.random.permutation(ks[7], n_pool)[:n_used].reshape(DEC_BATCH, n_pages).astype(jnp.int32)
    return {
        "x_prompt": jax.random.normal(ks[0], (BATCH, SEQ, D_MODEL), F32),
        "x_sample": jax.random.normal(ks[1], (DEC_BATCH, DEC_SEQ, D_MODEL), F32),
        "state_conv0": 0.5 * jax.random.normal(ks[2], (DEC_BATCH, CONV_STATE, D_CONV), F32),
        "cache_mla_ckv": jax.random.normal(ks[3], (n_pool, PAGE_SIZE, KV_LORA), F32),
        "cache_mla_krope": jax.random.normal(ks[4], (n_pool, PAGE_SIZE, MLA_ROPE), F32),
        "cache_sb_k": jax.random.normal(ks[5], (n_pool, PAGE_SIZE, SB_KV_HEADS, SB_HEAD_DIM), F32),
        "cache_sb_v": jax.random.normal(ks[6], (n_pool, PAGE_SIZE, SB_KV_HEADS, SB_HEAD_DIM), F32),
        "page_table": page_table,
        "ln_mix0": gain(ks[8], D_MODEL),
        "w_in0": w(ks[9], (D_MODEL, D_IN0), D_MODEL),
        "conv_w": w(ks[10], (CONV_WIDTH, D_CONV), CONV_WIDTH),
        "conv_b": bias(ks[11], D_CONV),
        "conv_ln_g": gain(ks[12], D_CONV),
        "conv_ln_b": bias(ks[13], D_CONV),
        "q_norm_g": gain(ks[14], Q_LORA),
        "kv_norm_g": gain(ks[15], KV_LORA),
        "w_uq": w(ks[16], (Q_LORA, MLA_HEADS, MLA_NOPE + MLA_ROPE), Q_LORA),
        "w_uk": w(ks[17], (KV_LORA, MLA_HEADS, MLA_NOPE), KV_LORA),
        "w_uv": w(ks[18], (KV_LORA, MLA_HEADS, MLA_V), KV_LORA),
        "w_out0": w(ks[19], (D_MIX0, D_MODEL), D_MIX0),
        "ln_ffn0": gain(ks[20], D_MODEL),
        "w1_d": w(ks[21], (D_MODEL, D_FF), D_MODEL),
        "w3_d": w(ks[22], (D_MODEL, D_FF), D_MODEL),
        "w2_d": w(ks[23], (D_FF, D_MODEL), D_FF),
        "ln_mix1": gain(ks[24], D_MODEL),
        "w_in1": w(ks[25], (D_MODEL, D_IN1), D_MODEL),
        "w_out1": w(ks[26], (D_MIX1, D_MODEL), D_MIX1),
        "ln_ffn1": gain(ks[27], D_MODEL),
        "w_router": w(ks[28], (D_MODEL, N_EXPERTS), D_MODEL),
        "w1_e": w(ks[29], (N_EXPERTS, D_MODEL, D_FF_EXPERT), D_MODEL),
        "w3_e": w(ks[30], (N_EXPERTS, D_MODEL, D_FF_EXPERT), D_MODEL),
        "w2_e": w(ks[31], (N_EXPERTS, D_FF_EXPERT, D_MODEL), D_FF_EXPERT),
        "ln_final": gain(ks[32], D_MODEL),
    }


def reference(x_prompt, x_sample, state_conv0, cache_mla_ckv, cache_mla_krope, cache_sb_k, cache_sb_v, page_table,
              ln_mix0, w_in0, conv_w, conv_b, conv_ln_g, conv_ln_b, q_norm_g, kv_norm_g, w_uq, w_uk, w_uv, w_out0,
              ln_ffn0, w1_d, w3_d, w2_d, ln_mix1, w_in1, w_out1, ln_ffn1, w_router, w1_e, w3_e, w2_e, ln_final):
    xp = x_prompt
    xs = x_sample
    pos_p = jnp.arange(xp.shape[1], dtype=F32)
    pos_s = PAST_LEN + jnp.arange(xs.shape[1], dtype=F32)
    conv_past_p = jnp.zeros((xp.shape[0], CONV_STATE, D_CONV), xp.dtype)

    def mla_sample(ql, qr, ckv, kr):
        return mla_sample_core(ql, qr, ckv, kr, cache_mla_ckv, cache_mla_krope, page_table)

    def sb_sample(q, k, v):
        return sb_sample_core(q, k, v, cache_sb_k, cache_sb_v, page_table)

    for layer in range(DEPTH):
        if layer % 2 == 0:
            xp, conv_p, ckv_p, krope_p = even_layer(
                xp, pos_p, conv_past_p, mla_prompt_core, ln_mix0, w_in0, conv_w, conv_b, conv_ln_g, conv_ln_b,
                q_norm_g, kv_norm_g, w_uq, w_uk, w_uv, w_out0, ln_ffn0, w1_d, w3_d, w2_d)
            xs, conv_s, ckv_s, krope_s = even_layer(
                xs, pos_s, state_conv0, mla_sample, ln_mix0, w_in0, conv_w, conv_b, conv_ln_g, conv_ln_b,
                q_norm_g, kv_norm_g, w_uq, w_uk, w_uv, w_out0, ln_ffn0, w1_d, w3_d, w2_d)
        else:
            xp, sbk_p, sbv_p = odd_layer(xp, sb_prompt_core, ln_mix1, w_in1, w_out1, ln_ffn1,
                                         w_router, w1_e, w3_e, w2_e)
            xs, sbk_s, sbv_s = odd_layer(xs, sb_sample, ln_mix1, w_in1, w_out1, ln_ffn1,
                                         w_router, w1_e, w3_e, w2_e)

    y_prompt = rmsnorm(xp, ln_final)
    y_sample = rmsnorm(xs, ln_final)
    return (y_prompt, y_sample, conv_p, ckv_p, krope_p, sbk_p, sbv_p, conv_s, ckv_s, krope_s, sbk_s, sbv_s)
```

```python
import functools

import jax
import jax.numpy as jnp
from jax import lax
from jax.experimental import pallas as pl
from jax.experimental.pallas import tpu as pltpu

F32 = jnp.float32
BF16 = jnp.bfloat16
EPS = 1e-6
ROPE_THETA = 10000.0
TOP_K = 2

LANES = 128
MXU_DIM = 256
VMEM_LIMIT = 56 * 1024 * 1024


def _tile(n, pref, align):
    t = min(pref, n) // align * align
    while t >= align:
        if n % t == 0:
            return t
        t -= align
    return n


def _params(*sem):
    return pltpu.CompilerParams(dimension_semantics=sem, vmem_limit_bytes=VMEM_LIMIT)


def _rms(x, g):
    return x * lax.rsqrt(jnp.mean(x * x, axis=-1, keepdims=True) + EPS) * g


def _dot(a, b):
    return jnp.dot(a, b, preferred_element_type=F32)


def _dot_nt(a, b):
    return lax.dot_general(a, b, (((1,), (1,)), ((), ())), preferred_element_type=F32)


def _norm_matmul_kernel(x_ref, g_ref, w_ref, o_ref, xn_ref):
    @pl.when(pl.program_id(1) == 0)
    def _():
        xn_ref[...] = _rms(x_ref[...], g_ref[...]).astype(BF16)

    o_ref[...] = _dot(xn_ref[...], w_ref[...]).astype(o_ref.dtype)


def _norm_matmul(x, g, w, name):
    m, k = x.shape
    n = w.shape[1]
    bm = _tile(m, 512, 8)
    bn = _tile(n, 640, LANES)
    return pl.pallas_call(
        _norm_matmul_kernel,
        grid=(m // bm, n // bn),
        in_specs=[pl.BlockSpec((bm, k), lambda i, j: (i, 0)),
                  pl.BlockSpec((1, k), lambda i, j: (0, 0)),
                  pl.BlockSpec((k, bn), lambda i, j: (0, j))],
        out_specs=pl.BlockSpec((bm, bn), lambda i, j: (i, j)),
        out_shape=jax.ShapeDtypeStruct((m, n), F32),
        scratch_shapes=[pltpu.VMEM((bm, k), BF16)],
        compiler_params=_params("parallel", "arbitrary"),
        name=name,
    )(x, g.reshape(1, k), w)


def _matmul_res_kernel(*refs, n_a):
    a_refs, w_refs = refs[:n_a], refs[n_a:2 * n_a]
    res_ref, o_ref = refs[2 * n_a], refs[2 * n_a + 1]
    acc = res_ref[...]
    for a_ref, w_ref in zip(a_refs, w_refs):
        acc = acc + _dot(a_ref[...], w_ref[...])
    o_ref[...] = acc


def _matmul_res(a_list, w_list, res, name):
    m, n = res.shape
    bm = _tile(m, 512, 8)
    bn = _tile(n, 1024, LANES)
    n_a = len(a_list)
    in_specs = ([pl.BlockSpec((bm, a.shape[1]), lambda i, j: (i, 0)) for a in a_list]
                + [pl.BlockSpec((w.shape[0], bn), lambda i, j: (0, j)) for w in w_list]
                + [pl.BlockSpec((bm, bn), lambda i, j: (i, j))])
    return pl.pallas_call(
        functools.partial(_matmul_res_kernel, n_a=n_a),
        grid=(m // bm, n // bn),
        in_specs=in_specs,
        out_specs=pl.BlockSpec((bm, bn), lambda i, j: (i, j)),
        out_shape=jax.ShapeDtypeStruct((m, n), F32),
        compiler_params=_params("parallel", "arbitrary"),
        name=name,
    )(*a_list, *w_list, res)


def _ffn_body(x_ref, g_ref, w1_ref, w3_ref, w2_ref, o_ref, xn_ref, f):
    @pl.when(f == 0)
    def _():
        xn_ref[...] = _rms(x_ref[...], g_ref[...]).astype(BF16)

    xn = xn_ref[...]
    h1 = _dot(xn, w1_ref[...])
    h3 = _dot(xn, w3_ref[...])
    h = (h1 * jax.nn.sigmoid(h1) * h3).astype(BF16)
    return _dot(h, w2_ref[...])


def _ffn_dense_kernel(x_ref, g_ref, w1_ref, w3_ref, w2_ref, o_ref, xn_ref):
    f = pl.program_id(1)
    y = _ffn_body(x_ref, g_ref, w1_ref, w3_ref, w2_ref, o_ref, xn_ref, f)

    @pl.when(f == 0)
    def _():
        o_ref[...] = x_ref[...] + y

    @pl.when(f > 0)
    def _():
        o_ref[...] += y


def _ffn_dense(x, g, w1, w3, w2, name):
    m, d = x.shape
    ff = w1.shape[1]
    bm = _tile(m, 512, 8)
    bf = _tile(ff, 512, LANES)
    return pl.pallas_call(
        _ffn_dense_kernel,
        grid=(m // bm, ff // bf),
        in_specs=[pl.BlockSpec((bm, d), lambda i, f: (i, 0)),
                  pl.BlockSpec((1, d), lambda i, f: (0, 0)),
                  pl.BlockSpec((d, bf), lambda i, f: (0, f)),
                  pl.BlockSpec((d, bf), lambda i, f: (0, f)),
                  pl.BlockSpec((bf, d), lambda i, f: (f, 0))],
        out_specs=pl.BlockSpec((bm, d), lambda i, f: (i, 0)),
        out_shape=jax.ShapeDtypeStruct((m, d), F32),
        scratch_shapes=[pltpu.VMEM((bm, d), BF16)],
        compiler_params=_params("parallel", "arbitrary"),
        name=name,
    )(x, g.reshape(1, d), w1, w3, w2)


def _ffn_expert_kernel(te_ref, tb_ref, tf_ref, x_ref, g_ref, gate_ref, w1_ref, w3_ref, w2_ref, o_ref,
                       xn_ref, *, n_f):
    i = pl.program_id(0)
    f = pl.program_id(1)

    @pl.when(jnp.logical_and(tb_ref[i] != i, f == 0))
    def _():
        o_ref[...] = jnp.zeros(o_ref.shape, F32)

    @pl.when(tb_ref[i] == i)
    def _():
        y = _ffn_body(x_ref, g_ref, w1_ref, w3_ref, w2_ref, o_ref, xn_ref, f)

        @pl.when(f == 0)
        def _():
            o_ref[...] = y

        @pl.when(f > 0)
        def _():
            o_ref[...] += y

        @pl.when(f == n_f - 1)
        def _():
            o_ref[...] = o_ref[...] * gate_ref[...]


def _ffn_experts(xs, g, row_gate, w1_e, w3_e, w2_e, tile_expert, tile_block, bm, name):
    r, d = xs.shape
    ff = w1_e.shape[2]
    bf = _tile(ff, 512, LANES)
    n_f = ff // bf
    n_tiles = r // bm
    tile_f = jnp.where(tile_block == jnp.arange(n_tiles, dtype=jnp.int32), 0, 1).astype(jnp.int32)

    def f_idx(i, f, tf):
        return jnp.where(tf[i] == 0, f, n_f - 1)

    grid_spec = pltpu.PrefetchScalarGridSpec(
        num_scalar_prefetch=3,
        grid=(n_tiles, n_f),
        in_specs=[pl.BlockSpec((bm, d), lambda i, f, te, tb, tf: (tb[i], 0)),
                  pl.BlockSpec((1, d), lambda i, f, te, tb, tf: (0, 0)),
                  pl.BlockSpec((bm, 1), lambda i, f, te, tb, tf: (tb[i], 0)),
                  pl.BlockSpec((None, d, bf), lambda i, f, te, tb, tf: (te[i], 0, f_idx(i, f, tf))),
                  pl.BlockSpec((None, d, bf), lambda i, f, te, tb, tf: (te[i], 0, f_idx(i, f, tf))),
                  pl.BlockSpec((None, bf, d), lambda i, f, te, tb, tf: (te[i], f_idx(i, f, tf), 0))],
        out_specs=pl.BlockSpec((bm, d), lambda i, f, te, tb, tf: (i, 0)),
        scratch_shapes=[pltpu.VMEM((bm, d), BF16)],
    )
    return pl.pallas_call(
        functools.partial(_ffn_expert_kernel, n_f=n_f),
        grid_spec=grid_spec,
        out_shape=jax.ShapeDtypeStruct((r, d), F32),
        compiler_params=_params("arbitrary", "arbitrary"),
        name=name,
    )(tile_expert, tile_block, tile_f, xs, g.reshape(1, d), row_gate, w1_e, w3_e, w2_e)


def _ln_silu(h, g, b):
    mu = jnp.mean(h, axis=-1, keepdims=True)
    hc = h - mu
    y = hc * lax.rsqrt(jnp.mean(hc * hc, axis=-1, keepdims=True) + EPS) * g + b
    return y * jax.nn.sigmoid(y)


def _conv_prompt_kernel(a_ref, gate_ref, w_ref, cb_ref, lg_ref, lb_ref, o_ref, st_ref, uext_ref, h_ref,
                        *, bt, width, halo, rc):
    t = pl.program_id(1)
    c = a_ref.shape[1]

    @pl.when(t == 0)
    def _():
        uext_ref[0:halo, :] = jnp.zeros((halo, c), F32)

    @pl.when(t > 0)
    def _():
        uext_ref[0:halo, :] = uext_ref[bt:bt + halo, :]

    uext_ref[halo:halo + bt, :] = a_ref[...] * jax.nn.sigmoid(gate_ref[...])

    off = halo - (width - 1)
    for r0 in range(0, bt, rc):
        for c0 in range(0, c, LANES):
            acc = jnp.zeros((rc, LANES), F32)
            for k in range(width):
                acc = acc + w_ref[k:k + 1, c0:c0 + LANES] * uext_ref[r0 + off + k:r0 + off + k + rc, c0:c0 + LANES]
            h_ref[r0:r0 + rc, c0:c0 + LANES] = acc
    o_ref[...] = _ln_silu(h_ref[...] + cb_ref[...], lg_ref[...], lb_ref[...]).astype(o_ref.dtype)

    @pl.when(t == pl.num_programs(1) - 1)
    def _():
        st_ref[...] = uext_ref[bt:bt + halo, :]


def _conv_prompt(h0, n_b, n_t, c, conv_w, conv_b, ln_g, ln_b):
    width = conv_w.shape[0]
    halo = -(-(width - 1) // 8) * 8
    bt = _tile(n_t, 256, 8)
    nt = n_t // bt
    out, st = pl.pallas_call(
        functools.partial(_conv_prompt_kernel, bt=bt, width=width, halo=halo, rc=_tile(bt, 64, 8)),
        grid=(n_b, nt),
        in_specs=[pl.BlockSpec((bt, c), lambda b, t: (b * nt + t, 0)),
                  pl.BlockSpec((bt, c), lambda b, t: (b * nt + t, 1)),
                  pl.BlockSpec((width, c), lambda b, t: (0, 0)),
                  pl.BlockSpec((1, c), lambda b, t: (0, 0)),
                  pl.BlockSpec((1, c), lambda b, t: (0, 0)),
                  pl.BlockSpec((1, c), lambda b, t: (0, 0))],
        out_specs=[pl.BlockSpec((bt, c), lambda b, t: (b * nt + t, 0)),
                   pl.BlockSpec((None, halo, c), lambda b, t: (b, 0, 0))],
        out_shape=[jax.ShapeDtypeStruct((n_b * n_t, c), BF16),
                   jax.ShapeDtypeStruct((n_b, halo, c), F32)],
        scratch_shapes=[pltpu.VMEM((bt + halo, c), F32), pltpu.VMEM((bt, c), F32)],
        compiler_params=_params("parallel", "arbitrary"),
        name="conv_prompt",
    )(h0, h0, conv_w, conv_b.reshape(1, c), ln_g.reshape(1, c), ln_b.reshape(1, c))
    return out, st


def _conv_sample_kernel(a_ref, gate_ref, st_ref, wst_ref, wnew_ref, cb_ref, lg_ref, lb_ref, o_ref, u_ref, h_ref,
                        *, bb, n_t):
    u = a_ref[...] * jax.nn.sigmoid(gate_ref[...])
    u_ref[...] = u
    for b in range(bb):
        past = st_ref[b]
        new = u[b * n_t:(b + 1) * n_t, :]
        for t in range(n_t):
            h_ref[b * n_t + t:b * n_t + t + 1, :] = (
                jnp.sum(past * wst_ref[t], axis=0, keepdims=True)
                + jnp.sum(new * wnew_ref[t], axis=0, keepdims=True))
    o_ref[...] = _ln_silu(h_ref[...] + cb_ref[...], lg_ref[...], lb_ref[...]).astype(o_ref.dtype)


def _conv_sample(h0, row0, n_b, n_t, c, state, conv_w, conv_b, ln_g, ln_b):
    width = conv_w.shape[0]
    n_st = width - 1
    j = jnp.arange(n_st + n_t)[None, :] - jnp.arange(n_t)[:, None]
    w_shift = jnp.where(((j >= 0) & (j < width))[..., None], conv_w[jnp.clip(j, 0, width - 1)], 0.0)
    bb = _tile(n_b, 8, 1)
    rows = bb * n_t
    blk0 = row0 // rows
    assert row0 % rows == 0 and rows % 8 == 0
    out, u = pl.pallas_call(
        functools.partial(_conv_sample_kernel, bb=bb, n_t=n_t),
        grid=(n_b // bb,),
        in_specs=[pl.BlockSpec((rows, c), lambda i: (blk0 + i, 0)),
                  pl.BlockSpec((rows, c), lambda i: (blk0 + i, 1)),
                  pl.BlockSpec((bb, n_st, c), lambda i: (i, 0, 0)),
                  pl.BlockSpec((n_t, n_st, c), lambda i: (0, 0, 0)),
                  pl.BlockSpec((n_t, n_t, c), lambda i: (0, 0, 0)),
                  pl.BlockSpec((1, c), lambda i: (0, 0)),
                  pl.BlockSpec((1, c), lambda i: (0, 0)),
                  pl.BlockSpec((1, c), lambda i: (0, 0))],
        out_specs=[pl.BlockSpec((rows, c), lambda i: (i, 0)),
                   pl.BlockSpec((rows, c), lambda i: (i, 0))],
        out_shape=[jax.ShapeDtypeStruct((n_b * n_t, c), BF16),
                   jax.ShapeDtypeStruct((n_b * n_t, c), F32)],
        scratch_shapes=[pltpu.VMEM((rows, c), F32)],
        compiler_params=_params("parallel"),
        name="conv_sample",
    )(h0, h0, state, w_shift[:, :n_st], w_shift[:, n_st:], conv_b.reshape(1, c), ln_g.reshape(1, c),
      ln_b.reshape(1, c))
    return out, u


def _rope_lanes(x, cs, sn, half):
    lane = lax.broadcasted_iota(jnp.int32, x.shape, 1)
    swapped = jnp.where(lane % (2 * half) < half, pltpu.roll(x, LANES - half, 1), pltpu.roll(x, half, 1))
    return x * cs + swapped * sn


def _mla_prep_kernel(cq_ref, ckv_ref, kr_ref, gq_ref, gkv_ref, wuq_ref, cs_ref, sn_ref,
                     q_ref, ckv_o_ref, kr_o_ref, *, n_heads, half, scale):
    cs, sn = cs_ref[...], sn_ref[...]
    q = _dot(_rms(cq_ref[...], gq_ref[...]).astype(BF16), wuq_ref[...])
    hw = 2 * LANES
    for h in range(n_heads):
        q_ref[:, h * hw:h * hw + LANES] = (q[:, h * hw:h * hw + LANES] * scale).astype(BF16)
        q_ref[:, h * hw + LANES:(h + 1) * hw] = (
            _rope_lanes(q[:, h * hw + LANES:(h + 1) * hw], cs, sn, half) * scale).astype(BF16)
    ckv_o_ref[...] = _rms(ckv_ref[...], gkv_ref[...])
    kr_o_ref[...] = _rope_lanes(kr_ref[...], cs, sn, half)


def _mla_prep(h0, col_q, q_lora, kv_lora, g_q, g_kv, wuq_pad, cos_t, sin_t, n_heads, half, scale):
    m = h0.shape[0]
    bm = _tile(m, 512, 8)
    assert col_q % q_lora == 0 and (col_q + q_lora) % kv_lora == 0 and (col_q + q_lora + kv_lora) % LANES == 0
    hw = 2 * LANES
    return pl.pallas_call(
        functools.partial(_mla_prep_kernel, n_heads=n_heads, half=half, scale=scale),
        grid=(m // bm,),
        in_specs=[pl.BlockSpec((bm, q_lora), lambda i: (i, col_q // q_lora)),
                  pl.BlockSpec((bm, kv_lora), lambda i: (i, (col_q + q_lora) // kv_lora)),
                  pl.BlockSpec((bm, LANES), lambda i: (i, (col_q + q_lora + kv_lora) // LANES)),
                  pl.BlockSpec((1, q_lora), lambda i: (0, 0)),
                  pl.BlockSpec((1, kv_lora), lambda i: (0, 0)),
                  pl.BlockSpec((q_lora, n_heads * hw), lambda i: (0, 0)),
                  pl.BlockSpec((bm, LANES), lambda i: (i, 0)),
                  pl.BlockSpec((bm, LANES), lambda i: (i, 0))],
        out_specs=[pl.BlockSpec((bm, n_heads * hw), lambda i: (i, 0)),
                   pl.BlockSpec((bm, kv_lora), lambda i: (i, 0)),
                   pl.BlockSpec((bm, LANES), lambda i: (i, 0))],
        out_shape=[jax.ShapeDtypeStruct((m, n_heads * hw), BF16),
                   jax.ShapeDtypeStruct((m, kv_lora), F32),
                   jax.ShapeDtypeStruct((m, LANES), F32)],
        compiler_params=_params("parallel"),
        name="mla_prep",
    )(h0, h0, h0, g_q.reshape(1, -1), g_kv.reshape(1, -1), wuq_pad, cos_t, sin_t)


def _mla_kv_kernel(ckv_ref, kr_ref, wuk_ref, wuv_ref, k_ref, v_ref, *, n_heads):
    c = ckv_ref[...].astype(BF16)
    kn = _dot(c, wuk_ref[...])
    v_ref[...] = _dot(c, wuv_ref[...]).astype(BF16)
    kr = kr_ref[...].astype(BF16)
    hw = 2 * LANES
    for h in range(n_heads):
        k_ref[:, h * hw:h * hw + LANES] = kn[:, h * LANES:(h + 1) * LANES].astype(BF16)
        k_ref[:, h * hw + LANES:(h + 1) * hw] = kr


def _mla_kv(ckv, kr_pad, m_p, wuk_flat, wuv_flat, n_heads):
    kv_lora = ckv.shape[1]
    bm = _tile(m_p, 512, 8)
    hw = 2 * LANES
    return pl.pallas_call(
        functools.partial(_mla_kv_kernel, n_heads=n_heads),
        grid=(m_p // bm,),
        in_specs=[pl.BlockSpec((bm, kv_lora), lambda i: (i, 0)),
                  pl.BlockSpec((bm, LANES), lambda i: (i, 0)),
                  pl.BlockSpec(wuk_flat.shape, lambda i: (0, 0)),
                  pl.BlockSpec(wuv_flat.shape, lambda i: (0, 0))],
        out_specs=[pl.BlockSpec((bm, n_heads * hw), lambda i: (i, 0)),
                   pl.BlockSpec((bm, n_heads * LANES), lambda i: (i, 0))],
        out_shape=[jax.ShapeDtypeStruct((m_p, n_heads * hw), BF16),
                   jax.ShapeDtypeStruct((m_p, n_heads * LANES), BF16)],
        compiler_params=_params("parallel"),
        name="mla_kv_expand",
    )(ckv, kr_pad, wuk_flat, wuv_flat)


def _head_matmul_kernel(a_ref, w_ref, o_ref):
    o_ref[...] = _dot(a_ref[...].astype(BF16), w_ref[...]).astype(o_ref.dtype)


def _head_matmul(a, row0, m, col0, col_stride, w, out_dtype, name):
    n_heads, k, n = w.shape
    assert col0 % k == 0 and col_stride % k == 0 and row0 % m == 0
    return pl.pallas_call(
        _head_matmul_kernel,
        grid=(n_heads,),
        in_specs=[pl.BlockSpec((m, k), lambda h: (row0 // m, (col0 + h * col_stride) // k)),
                  pl.BlockSpec((None, k, n), lambda h: (h, 0, 0))],
        out_specs=pl.BlockSpec((m, n), lambda h: (0, h)),
        out_shape=jax.ShapeDtypeStruct((m, n_heads * n), out_dtype),
        compiler_params=_params("parallel"),
        name=name,
    )(a, w)


def _mla_flash_kernel(q_ref, k_ref, v_ref, o_ref, m_ref, l_ref, acc_ref, *, blk):
    qi = pl.program_id(2)
    q = q_ref[...]
    m_ref[...] = jnp.full(m_ref.shape, -jnp.inf, F32)
    l_ref[...] = jnp.zeros(l_ref.shape, F32)
    acc_ref[...] = jnp.zeros(acc_ref.shape, F32)

    def step(kb, masked):
        r0 = pl.multiple_of(kb * blk, blk)
        s = _dot_nt(q, k_ref[pl.ds(r0, blk), :])
        if masked:
            row = lax.broadcasted_iota(jnp.int32, s.shape, 0)
            col = lax.broadcasted_iota(jnp.int32, s.shape, 1)
            s = jnp.where(col <= row, s, -jnp.inf)
        m_old = m_ref[...]
        m_new = jnp.maximum(m_old, jnp.max(s, axis=-1, keepdims=True))
        p = jnp.exp(s - m_new)
        corr = jnp.exp(m_old - m_new)
        l_ref[...] = l_ref[...] * corr + jnp.sum(p, axis=-1, keepdims=True)
        acc_ref[...] = acc_ref[...] * corr + _dot(p.astype(BF16), v_ref[pl.ds(r0, blk), :])
        m_ref[...] = m_new

    step(qi, True)

    def body(kb, carry):
        step(kb, False)
        return carry

    lax.fori_loop(0, qi, body, 0)
    o_ref[...] = (acc_ref[...] / l_ref[...]).astype(o_ref.dtype)


def _mla_flash(q_pad, k_pad, v, n_b, n_t, n_heads):
    blk = _tile(n_t, 256, 8)
    nq = n_t // blk
    hw = 2 * LANES
    return pl.pallas_call(
        functools.partial(_mla_flash_kernel, blk=blk),
        grid=(n_b, n_heads, nq),
        in_specs=[pl.BlockSpec((blk, hw), lambda b, h, i: (b * nq + i, h)),
                  pl.BlockSpec((n_t, hw), lambda b, h, i: (b, h)),
                  pl.BlockSpec((n_t, LANES), lambda b, h, i: (b, h))],
        out_specs=pl.BlockSpec((blk, LANES), lambda b, h, i: (b * nq + i, h)),
        out_shape=jax.ShapeDtypeStruct((n_b * n_t, n_heads * LANES), BF16),
        scratch_shapes=[pltpu.VMEM((blk, 1), F32), pltpu.VMEM((blk, 1), F32), pltpu.VMEM((blk, LANES), F32)],
        compiler_params=_params("parallel", "parallel", "arbitrary"),
        name="mla_prompt_attn",
    )(q_pad, k_pad, v)


def _first_slot(b):
    return b % 2


def _loop_slot(s):
    return 2 + (s - 1) % 2


def _mla_decode_kernel(pt_ref, ql_ref, qr_ref, cnew_ref, rnew_ref, ckv_hbm, kr_hbm, o_ref,
                       cbuf, rbuf, csem, rsem, m_ref, l_ref, acc_ref, *, n_pg, n_steps, n_t, n_heads, rope):
    b = pl.program_id(0)
    ql = ql_ref[...]
    qr = qr_ref[...][:, :rope]

    def copies(bb, step, slot):
        out = []
        for g in range(n_pg):
            pg = pt_ref[bb, step * n_pg + g]
            out.append(pltpu.make_async_copy(ckv_hbm.at[pg], cbuf.at[slot, g], csem.at[slot]))
            out.append(pltpu.make_async_copy(kr_hbm.at[pg], rbuf.at[slot, g], rsem.at[slot]))
        return out

    def start(bb, step, slot):
        for cp in copies(bb, step, slot):
            cp.start()

    def wait(bb, step, slot):
        for cp in copies(bb, step, slot):
            cp.wait()

    @pl.when(b == 0)
    def _():
        start(b, 0, _first_slot(b))

    @pl.when(b + 1 < pl.num_programs(0))
    def _():
        start(b + 1, 0, _first_slot(b + 1))

    if n_steps > 1:
        start(b, 1, _loop_slot(1))

    qlf, qrf = ql.astype(F32), qr_ref[...].astype(F32)
    t_row = lax.broadcasted_iota(jnp.int32, (ql.shape[0], 1), 0) // n_heads
    m = jnp.full((ql.shape[0], 1), -jnp.inf, F32)
    l = jnp.zeros((ql.shape[0], 1), F32)
    acc = jnp.zeros(acc_ref.shape, F32)
    for t in range(n_t):
        c = cnew_ref[t:t + 1, :]
        s = (jnp.sum(qlf * c, axis=-1, keepdims=True)
             + jnp.sum(qrf * rnew_ref[t:t + 1, :], axis=-1, keepdims=True))
        s = jnp.where(t <= t_row, s, -jnp.inf)
        m_new = jnp.maximum(m, s)
        corr = jnp.exp(m - m_new)
        p = jnp.exp(s - m_new)
        l = l * corr + p
        acc = acc * corr + p * c
        m = m_new
    m_ref[...] = m
    l_ref[...] = l
    acc_ref[...] = acc

    def compute(slot):
        ckv = [cbuf[slot, g].astype(BF16) for g in range(n_pg)]
        s = jnp.concatenate(
            [_dot_nt(ql, c) + _dot_nt(qr, rbuf[slot, g].astype(BF16)) for g, c in enumerate(ckv)], axis=-1)
        m_old = m_ref[...]
        m_new = jnp.maximum(m_old, jnp.max(s, axis=-1, keepdims=True))
        p = jnp.exp(s - m_new)
        corr = jnp.exp(m_old - m_new)
        l_ref[...] = l_ref[...] * corr + jnp.sum(p, axis=-1, keepdims=True)
        acc = acc_ref[...] * corr
        pg = ckv[0].shape[0]
        for g, c in enumerate(ckv):
            acc = acc + _dot(p[:, g * pg:(g + 1) * pg].astype(BF16), c)
        acc_ref[...] = acc
        m_ref[...] = m_new

    wait(b, 0, _first_slot(b))
    compute(_first_slot(b))

    def body(s, carry):
        @pl.when(s + 1 < n_steps)
        def _():
            start(b, s + 1, _loop_slot(s + 1))

        wait(b, s, _loop_slot(s))
        compute(_loop_slot(s))
        return carry

    lax.fori_loop(1, n_steps, body, 0)
    o_ref[...] = acc_ref[...] / l_ref[...]


def _mla_decode(page_table, q_lat, q_rope, ckv_new, kr_new, cache_ckv, cache_kr, n_heads):
    n_b, rows, c = q_lat.shape
    n_t = ckv_new.shape[1]
    n_pages = page_table.shape[1]
    page, rope = cache_kr.shape[1], cache_kr.shape[2]
    n_pg = _tile(n_pages, 8, 1)
    grid_spec = pltpu.PrefetchScalarGridSpec(
        num_scalar_prefetch=1,
        grid=(n_b,),
        in_specs=[pl.BlockSpec((None, rows, c), lambda b, pt: (b, 0, 0)),
                  pl.BlockSpec((None, rows, LANES), lambda b, pt: (b, 0, 0)),
                  pl.BlockSpec((None, n_t, c), lambda b, pt: (b, 0, 0)),
                  pl.BlockSpec((None, n_t, LANES), lambda b, pt: (b, 0, 0)),
                  pl.BlockSpec(memory_space=pl.ANY),
                  pl.BlockSpec(memory_space=pl.ANY)],
        out_specs=pl.BlockSpec((None, rows, c), lambda b, pt: (b, 0, 0)),
        scratch_shapes=[pltpu.VMEM((4, n_pg, page, c), F32), pltpu.VMEM((4, n_pg, page, rope), F32),
                        pltpu.SemaphoreType.DMA((4,)), pltpu.SemaphoreType.DMA((4,)),
                        pltpu.VMEM((rows, 1), F32), pltpu.VMEM((rows, 1), F32), pltpu.VMEM((rows, c), F32)],
    )
    return pl.pallas_call(
        functools.partial(_mla_decode_kernel, n_pg=n_pg, n_steps=n_pages // n_pg, n_t=n_t, n_heads=n_heads,
                          rope=rope),
        grid_spec=grid_spec,
        out_shape=jax.ShapeDtypeStruct((n_b, rows, c), F32),
        compiler_params=_params("arbitrary"),
        name="mla_sample_attn",
    )(page_table, q_lat, q_rope, ckv_new, kr_new, cache_ckv, cache_kr)


def _sb_block(q, k, v, u_ref, tail, mask):
    z = _dot_nt(q, k)
    sp = jnp.log1p(jnp.exp(-jnp.abs(z)))
    log_beta = jnp.minimum(z, 0.0) - sp
    log_1m = jnp.minimum(-z, 0.0) - sp
    if mask is not None:
        log_1m = jnp.where(mask, log_1m, 0.0)
    hi = log_1m.astype(BF16)
    lo = (log_1m - hi.astype(F32)).astype(BF16)
    later = _dot(hi, u_ref[...]) + _dot(lo, u_ref[...]) + tail
    w = jnp.exp(log_beta + later)
    if mask is not None:
        w = jnp.where(mask, w, 0.0)
    return _dot(w.astype(BF16), v), tail + jnp.sum(log_1m, axis=-1, keepdims=True)


def _suffix_matrix(n):
    j = jnp.arange(n)
    return (j[:, None] > j[None, :]).astype(BF16)


def _sb_prompt_kernel(q_ref, k_ref, v_ref, u_ref, o_ref, acc_ref, tail_ref, *, blk, scale):
    qi = pl.program_id(2)
    q = (q_ref[...] * scale).astype(BF16)

    def kv(kb):
        r0 = pl.multiple_of(kb * blk, blk)
        return k_ref[pl.ds(r0, blk), :].astype(BF16), v_ref[pl.ds(r0, blk), :].astype(BF16)

    row = lax.broadcasted_iota(jnp.int32, (blk, blk), 0)
    col = lax.broadcasted_iota(jnp.int32, (blk, blk), 1)
    k, v = kv(qi)
    acc, tail = _sb_block(q, k, v, u_ref, jnp.zeros((blk, 1), F32), col < row)
    acc_ref[...] = acc
    tail_ref[...] = tail

    def body(n, carry):
        k, v = kv(qi - 1 - n)
        acc, tail = _sb_block(q, k, v, u_ref, tail_ref[...], None)
        acc_ref[...] += acc
        tail_ref[...] = tail
        return carry

    lax.fori_loop(0, qi, body, 0)
    o_ref[...] = acc_ref[...].astype(o_ref.dtype)


def _sb_prompt(h1, n_b, n_t, n_heads, n_kv, dh, scale):
    blk = _tile(n_t, 256, 8)
    nq = n_t // blk
    group = n_heads // n_kv
    return pl.pallas_call(
        functools.partial(_sb_prompt_kernel, blk=blk, scale=scale),
        grid=(n_b, n_heads, nq),
        in_specs=[pl.BlockSpec((blk, dh), lambda b, h, i: (b * nq + i, h)),
                  pl.BlockSpec((n_t, dh), lambda b, h, i: (b, n_heads + h // group)),
                  pl.BlockSpec((n_t, dh), lambda b, h, i: (b, n_heads + n_kv + h // group)),
                  pl.BlockSpec((blk, blk), lambda b, h, i: (0, 0))],
        out_specs=pl.BlockSpec((blk, dh), lambda b, h, i: (b * nq + i, h)),
        out_shape=jax.ShapeDtypeStruct((n_b * n_t, n_heads * dh), BF16),
        scratch_shapes=[pltpu.VMEM((blk, dh), F32), pltpu.VMEM((blk, 1), F32)],
        compiler_params=_params("parallel", "parallel", "arbitrary"),
        name="sb_prompt_attn",
    )(h1, h1, h1, _suffix_matrix(blk))


SB_DEAD_TAIL = -104.0


def _sb_decode_kernel(pt_ref, q_ref, kn_ref, vn_ref, u_ref, k_hbm, v_hbm, o_ref,
                      kbuf, vbuf, ksem, vsem, acc_ref, tail_ref, alive_ref,
                      *, n_pg, n_steps, n_pages, n_t, n_kv, group, dh, scale):
    b = pl.program_id(0)

    def copies(bb, step, slot):
        out = []
        for g in range(n_pg):
            pg = pt_ref[bb, n_pages - 1 - (step * n_pg + g)]
            out.append(pltpu.make_async_copy(k_hbm.at[pg], kbuf.at[slot, g], ksem.at[slot]))
            out.append(pltpu.make_async_copy(v_hbm.at[pg], vbuf.at[slot, g], vsem.at[slot]))
        return out

    def start(bb, step, slot):
        for cp in copies(bb, step, slot):
            cp.start()

    def wait(bb, step, slot):
        for cp in copies(bb, step, slot):
            cp.wait()

    @pl.when(b == 0)
    def _():
        start(b, 0, _first_slot(b))

    @pl.when(b + 1 < pl.num_programs(0))
    def _():
        start(b + 1, 0, _first_slot(b + 1))

    if n_steps > 1:
        start(b, 1, _loop_slot(1))

    t_row = lax.broadcasted_iota(jnp.int32, (n_t * group, 1), 0) // group
    for g in range(n_kv):
        q = (q_ref[g] * scale).astype(BF16).astype(F32)
        tail = jnp.zeros((n_t * group, 1), F32)
        acc = jnp.zeros((n_t * group, dh), F32)
        for t in reversed(range(n_t)):
            z = jnp.sum(q * kn_ref[t:t + 1, g * dh:(g + 1) * dh], axis=-1, keepdims=True)
            sp = jnp.log1p(jnp.exp(-jnp.abs(z)))
            mask = t < t_row
            w = jnp.where(mask, jnp.exp(jnp.minimum(z, 0.0) - sp + tail), 0.0)
            acc = acc + w * vn_ref[t:t + 1, g * dh:(g + 1) * dh]
            tail = tail + jnp.where(mask, jnp.minimum(-z, 0.0) - sp, 0.0)
        acc_ref[g] = acc
        tail_ref[g] = tail
    alive_ref[0] = 1

    def compute(slot):
        for g in range(n_pg):
            @pl.when(alive_ref[0] == 1)
            def _():
                for grp in range(n_kv):
                    q = (q_ref[grp] * scale).astype(BF16)
                    d, tail = _sb_block(q, kbuf[slot, g][:, grp * dh:(grp + 1) * dh].astype(BF16),
                                        vbuf[slot, g][:, grp * dh:(grp + 1) * dh].astype(BF16),
                                        u_ref, tail_ref[grp], None)
                    acc_ref[grp] += d
                    tail_ref[grp] = tail
                alive_ref[0] = (jnp.max(tail_ref[...]) > SB_DEAD_TAIL).astype(jnp.int32)

    wait(b, 0, _first_slot(b))
    compute(_first_slot(b))

    if n_steps > 1:
        def cond(c):
            return jnp.logical_and(c[0] < n_steps, c[1] == 1)

        def body(c):
            s = c[0]

            @pl.when(s + 1 < n_steps)
            def _():
                start(b, s + 1, _loop_slot(s + 1))

            wait(b, s, _loop_slot(s))
            compute(_loop_slot(s))
            return s + 1, alive_ref[0]

        s_end, _ = lax.while_loop(cond, body, (jnp.int32(1), alive_ref[0]))

        @pl.when(s_end < n_steps)
        def _():
            wait(b, s_end, _loop_slot(s_end))

    o_ref[...] = acc_ref[...]


def _sb_decode(page_table, q, k_new, v_new, cache_k, cache_v, scale):
    n_b, n_kv, rows, dh = q.shape
    n_t = k_new.shape[1]
    group = rows // n_t
    n_pages = page_table.shape[1]
    page = cache_k.shape[1]
    n_pg = _tile(n_pages, 4, 1)
    grid_spec = pltpu.PrefetchScalarGridSpec(
        num_scalar_prefetch=1,
        grid=(n_b,),
        in_specs=[pl.BlockSpec((None, n_kv, rows, dh), lambda b, pt: (b, 0, 0, 0)),
                  pl.BlockSpec((None, n_t, n_kv * dh), lambda b, pt: (b, 0, 0)),
                  pl.BlockSpec((None, n_t, n_kv * dh), lambda b, pt: (b, 0, 0)),
                  pl.BlockSpec((page, page), lambda b, pt: (0, 0)),
                  pl.BlockSpec(memory_space=pl.ANY),
                  pl.BlockSpec(memory_space=pl.ANY)],
        out_specs=pl.BlockSpec((None, n_kv, rows, dh), lambda b, pt: (b, 0, 0, 0)),
        scratch_shapes=[pltpu.VMEM((4, n_pg, page, n_kv * dh), F32), pltpu.VMEM((4, n_pg, page, n_kv * dh), F32),
                        pltpu.SemaphoreType.DMA((4,)), pltpu.SemaphoreType.DMA((4,)),
                        pltpu.VMEM((n_kv, rows, dh), F32), pltpu.VMEM((n_kv, rows, 1), F32),
                        pltpu.SMEM((1,), jnp.int32)],
    )
    return pl.pallas_call(
        functools.partial(_sb_decode_kernel, n_pg=n_pg, n_steps=n_pages // n_pg, n_pages=n_pages, n_t=n_t,
                          n_kv=n_kv, group=group, dh=dh, scale=scale),
        grid_spec=grid_spec,
        out_shape=jax.ShapeDtypeStruct((n_b, n_kv, rows, dh), F32),
        compiler_params=_params("arbitrary"),
        name="sb_sample_attn",
    )(page_table, q, k_new, v_new, _suffix_matrix(page), cache_k, cache_v)


def _router_kernel(x_ref, g_ref, wr_ref, gate_ref, sel_ref, *, n_exp):
    xn = _rms(x_ref[...], g_ref[...])
    logits = jnp.concatenate(
        [jnp.sum(xn * wr_ref[e:e + 1, :], axis=-1, keepdims=True) for e in range(n_exp)], axis=-1)
    e_id = lax.broadcasted_iota(jnp.int32, logits.shape, 1)
    m1 = jnp.max(logits, axis=-1, keepdims=True)
    i1 = jnp.min(jnp.where(logits == m1, e_id, n_exp), axis=-1, keepdims=True)
    rest = jnp.where(e_id == i1, -jnp.inf, logits)
    m2 = jnp.max(rest, axis=-1, keepdims=True)
    i2 = jnp.min(jnp.where(rest == m2, e_id, n_exp), axis=-1, keepdims=True)
    e2 = jnp.exp(m2 - m1)
    den = 1.0 + e2
    gate_ref[...] = jnp.where(e_id == i1, 1.0 / den, 0.0) + jnp.where(e_id == i2, e2 / den, 0.0)
    sel_ref[...] = jnp.where((e_id == i1) | (e_id == i2), 1.0, 0.0)


def _router(x, g, w_router):
    m, d = x.shape
    n_exp = w_router.shape[1]
    bm = _tile(m, 512, 8)
    return pl.pallas_call(
        functools.partial(_router_kernel, n_exp=n_exp),
        grid=(m // bm,),
        in_specs=[pl.BlockSpec((bm, d), lambda i: (i, 0)),
                  pl.BlockSpec((1, d), lambda i: (0, 0)),
                  pl.BlockSpec((n_exp, d), lambda i: (0, 0))],
        out_specs=[pl.BlockSpec((bm, n_exp), lambda i: (i, 0))] * 2,
        out_shape=[jax.ShapeDtypeStruct((m, n_exp), F32)] * 2,
        compiler_params=_params("parallel"),
        name="moe_router",
    )(x, g.reshape(1, d), w_router.T)


def _row_copy(src_hbm, row, dst, slot, sem):
    return pltpu.make_async_copy(src_hbm.at[pl.ds(row, 1), :], dst.at[pl.ds(slot, 1), :], sem)


def _gather_rows_kernel(idx_ref, x_hbm, o_ref, sem, *, bm):
    i = pl.program_id(0)

    def start(r, carry):
        _row_copy(x_hbm, idx_ref[i, r], o_ref, r, sem).start()
        return carry

    def wait(r, carry):
        _row_copy(x_hbm, idx_ref[i, r], o_ref, r, sem).wait()
        return carry

    lax.fori_loop(0, bm, start, 0)
    lax.fori_loop(0, bm, wait, 0)


def _gather_rows(x, idx, bm):
    n_tiles = idx.shape[0]
    d = x.shape[1]
    grid_spec = pltpu.PrefetchScalarGridSpec(
        num_scalar_prefetch=1,
        grid=(n_tiles,),
        in_specs=[pl.BlockSpec(memory_space=pl.ANY)],
        out_specs=pl.BlockSpec((bm, d), lambda i, idx: (i, 0)),
        scratch_shapes=[pltpu.SemaphoreType.DMA],
    )
    return pl.pallas_call(
        functools.partial(_gather_rows_kernel, bm=bm),
        grid_spec=grid_spec,
        out_shape=jax.ShapeDtypeStruct((n_tiles * bm, d), x.dtype),
        compiler_params=_params("arbitrary"),
        name="moe_gather",
    )(idx, x)


def _combine_kernel(p0_ref, p1_ref, x_ref, g_ref, y_hbm, o_ref, buf_ref, sem, *, bm):
    i = pl.program_id(0)

    def start(r, carry):
        _row_copy(y_hbm, p0_ref[i, r], buf_ref.at[0], r, sem).start()
        _row_copy(y_hbm, p1_ref[i, r], buf_ref.at[1], r, sem).start()
        return carry

    def wait(r, carry):
        _row_copy(y_hbm, p0_ref[i, r], buf_ref.at[0], r, sem).wait()
        _row_copy(y_hbm, p1_ref[i, r], buf_ref.at[1], r, sem).wait()
        return carry

    lax.fori_loop(0, bm, start, 0)
    lax.fori_loop(0, bm, wait, 0)
    o_ref[...] = _rms(x_ref[...] + (buf_ref[0] + buf_ref[1]), g_ref[...])


def _combine_norm(x, g, ys, pos0, pos1, bm):
    m, d = x.shape
    grid_spec = pltpu.PrefetchScalarGridSpec(
        num_scalar_prefetch=2,
        grid=(m // bm,),
        in_specs=[pl.BlockSpec((bm, d), lambda i, p0, p1: (i, 0)),
                  pl.BlockSpec((1, d), lambda i, p0, p1: (0, 0)),
                  pl.BlockSpec(memory_space=pl.ANY)],
        out_specs=pl.BlockSpec((bm, d), lambda i, p0, p1: (i, 0)),
        scratch_shapes=[pltpu.VMEM((2, bm, d), F32), pltpu.SemaphoreType.DMA],
    )
    return pl.pallas_call(
        functools.partial(_combine_kernel, bm=bm),
        grid_spec=grid_spec,
        out_shape=jax.ShapeDtypeStruct((m, d), F32),
        compiler_params=_params("arbitrary"),
        name="moe_combine_norm",
    )(pos0, pos1, x, g.reshape(1, d), ys)


def _moe_final(x, g_ffn, g_final, w_router, w1_e, w3_e, w2_e):
    m, d = x.shape
    n_exp = w_router.shape[1]
    bm = _tile(m, 512, 8)
    gate, sel = _router(x, g_ffn, w_router)
    sel = sel > 0.0
    cnt = jnp.cumsum(sel.astype(jnp.int32), axis=0)
    n_e = cnt[-1]
    padded = (n_e + bm - 1) // bm * bm
    ends = jnp.cumsum(padded)
    dest = jnp.where(sel, (ends - padded)[None, :] + cnt - 1, -1)
    n_rows = (TOP_K * m + n_exp * (bm - 1)) // bm * bm
    n_tiles = n_rows // bm
    tok = jnp.broadcast_to(jnp.arange(m, dtype=jnp.int32)[:, None], dest.shape)
    dest_w = jnp.where(sel, dest, n_rows)
    row_tok = jnp.zeros((n_rows,), jnp.int32).at[dest_w].set(tok, mode="drop")
    row_gate = jnp.zeros((n_rows,), F32).at[dest_w].set(gate, mode="drop")
    n_used = jnp.maximum(ends[-1] // bm, 1)
    tile_block = jnp.minimum(jnp.arange(n_tiles, dtype=jnp.int32), n_used - 1)
    tile_expert = jnp.minimum(jnp.sum(tile_block[:, None] * bm >= ends[None, :], axis=1), n_exp - 1)
    first = jnp.argmax(sel, axis=1)
    last = n_exp - 1 - jnp.argmax(sel[:, ::-1], axis=1)
    pos0 = jnp.take_along_axis(dest, first[:, None], axis=1)[:, 0]
    pos1 = jnp.take_along_axis(dest, last[:, None], axis=1)[:, 0]

    xs = _gather_rows(x, row_tok.reshape(n_tiles, bm), bm)
    ys = _ffn_experts(xs, g_ffn, row_gate.reshape(n_rows, 1), w1_e, w3_e, w2_e,
                      tile_expert.astype(jnp.int32), tile_block, bm, "moe_experts")
    bc = _tile(m, 256, 8)
    return _combine_norm(x, g_final, ys, pos0.reshape(m // bc, bc).astype(jnp.int32),
                         pos1.reshape(m // bc, bc).astype(jnp.int32), bc)


def kernel(x_prompt, x_sample, state_conv0, cache_mla_ckv, cache_mla_krope, cache_sb_k, cache_sb_v, page_table,
           ln_mix0, w_in0, conv_w, conv_b, conv_ln_g, conv_ln_b, q_norm_g, kv_norm_g, w_uq, w_uk, w_uv, w_out0,
           ln_ffn0, w1_d, w3_d, w2_d, ln_mix1, w_in1, w_out1, ln_ffn1, w_router, w1_e, w3_e, w2_e, ln_final):
    n_bp, n_tp, d = x_prompt.shape
    n_bs, n_ts, _ = x_sample.shape
    m_p, m_s = n_bp * n_tp, n_bs * n_ts
    c_conv = conv_w.shape[1]
    q_lora, n_hm, qk_dim = w_uq.shape
    kv_lora, _, nope = w_uk.shape
    v_dim = w_uv.shape[2]
    rope = qk_dim - nope
    half = rope // 2
    page = cache_mla_ckv.shape[1]
    past_len = page_table.shape[1] * page
    n_kv, dh = cache_sb_k.shape[2], cache_sb_k.shape[3]
    n_hs = w_out1.shape[0] // dh
    group = n_hs // n_kv
    assert nope == LANES and v_dim == LANES and dh == LANES and rope <= LANES

    x = jnp.concatenate([x_prompt.reshape(m_p, d), x_sample.reshape(m_s, d)], axis=0)

    d_in0 = w_in0.shape[1]
    d_in0_pad = -(-d_in0 // 640) * 640 if d_in0 > 640 else -(-d_in0 // LANES) * LANES
    w_in0_p = jnp.pad(w_in0, ((0, 0), (0, d_in0_pad - d_in0))).astype(BF16)
    hw = 2 * LANES
    wuq_pad = jnp.pad(w_uq, ((0, 0), (0, 0), (0, hw - qk_dim))).reshape(q_lora, n_hm * hw).astype(BF16)
    wuk_flat = w_uk.reshape(kv_lora, n_hm * nope).astype(BF16)
    wuv_flat = w_uv.reshape(kv_lora, n_hm * v_dim).astype(BF16)
    wuk_t = jnp.transpose(w_uk, (1, 2, 0)).astype(BF16)
    wuv_h = jnp.transpose(w_uv, (1, 0, 2)).astype(BF16)
    w_out0_b = w_out0.astype(BF16)
    w1_db, w3_db, w2_db = w1_d.astype(BF16), w3_d.astype(BF16), w2_d.astype(BF16)
    w_in1_b, w_out1_b = w_in1.astype(BF16), w_out1.astype(BF16)
    w1_eb, w3_eb, w2_eb = w1_e.astype(BF16), w3_e.astype(BF16), w2_e.astype(BF16)

    pos = jnp.concatenate([jnp.tile(jnp.arange(n_tp, dtype=F32), n_bp),
                           jnp.tile(past_len + jnp.arange(n_ts, dtype=F32), n_bs)])
    freqs = ROPE_THETA ** (-jnp.arange(half, dtype=F32) / half)
    ang = pos[:, None] * freqs[None, :]
    zpad = jnp.zeros((m_p + m_s, LANES - rope), F32)
    cos_t = jnp.concatenate([jnp.cos(ang), jnp.cos(ang), zpad], axis=1)
    sin_t = jnp.concatenate([-jnp.sin(ang), jnp.sin(ang), zpad], axis=1)

    h0 = _norm_matmul(x, ln_mix0, w_in0_p, "in_proj0")
    conv_p, st_p = _conv_prompt(h0, n_bp, n_tp, c_conv, conv_w, conv_b, conv_ln_g, conv_ln_b)
    conv_s, u_s = _conv_sample(h0, m_p, n_bs, n_ts, c_conv, state_conv0, conv_w, conv_b, conv_ln_g, conv_ln_b)
    n_state = conv_w.shape[0] - 1
    conv_state_p = st_p[:, st_p.shape[1] - n_state:, :]
    conv_state_s = jnp.concatenate([state_conv0, u_s.reshape(n_bs, n_ts, c_conv)], axis=1)[:, -n_state:, :]

    mla_scale = float(qk_dim) ** -0.5
    q_pad, ckv, kr_pad = _mla_prep(h0, 2 * c_conv, q_lora, kv_lora, q_norm_g, kv_norm_g, wuq_pad, cos_t, sin_t,
                                   n_hm, half, mla_scale)
    k_pad, v_p = _mla_kv(ckv, kr_pad, m_p, wuk_flat, wuv_flat, n_hm)
    o_p = _mla_flash(q_pad, k_pad, v_p, n_bp, n_tp, n_hm)

    q_lat = _head_matmul(q_pad, m_p, m_s, 0, hw, wuk_t, BF16, "mla_q_absorb")
    q_lat = q_lat.reshape(n_bs, n_ts * n_hm, kv_lora)
    q_rope_s = q_pad[m_p:].reshape(n_bs, n_ts * n_hm, hw)[:, :, LANES:]
    o_lat = _mla_decode(page_table, q_lat, q_rope_s, ckv[m_p:].reshape(n_bs, n_ts, kv_lora),
                        kr_pad[m_p:].reshape(n_bs, n_ts, LANES), cache_mla_ckv, cache_mla_krope, n_hm)
    o_s = _head_matmul(o_lat.reshape(m_s, n_hm * kv_lora), 0, m_s, 0, kv_lora, wuv_h, BF16, "mla_v_absorb")

    conv_out = jnp.concatenate([conv_p, conv_s], axis=0)
    mla_out = jnp.concatenate([o_p, o_s], axis=0)
    x = _matmul_res([conv_out, mla_out], [w_out0_b[:c_conv], w_out0_b[c_conv:]], x, "out_proj0")
    x = _ffn_dense(x, ln_ffn0, w1_db, w3_db, w2_db, "ffn0")

    h1 = _norm_matmul(x, ln_mix1, w_in1_b, "in_proj1")
    sb_scale = float(dh) ** -0.5
    so_p = _sb_prompt(h1, n_bp, n_tp, n_hs, n_kv, dh, sb_scale)
    d_q = n_hs * dh
    h1_s = h1[m_p:]
    q_s = jnp.transpose(h1_s[:, :d_q].reshape(n_bs, n_ts, n_kv, group, dh), (0, 2, 1, 3, 4))
    q_s = q_s.reshape(n_bs, n_kv, n_ts * group, dh)
    k_s = h1_s[:, d_q:d_q + n_kv * dh].reshape(n_bs, n_ts, n_kv * dh)
    v_s = h1_s[:, d_q + n_kv * dh:].reshape(n_bs, n_ts, n_kv * dh)
    so_s = _sb_decode(page_table, q_s, k_s, v_s, cache_sb_k.reshape(-1, page, n_kv * dh),
                      cache_sb_v.reshape(-1, page, n_kv * dh), sb_scale)
    so_s = jnp.transpose(so_s.reshape(n_bs, n_kv, n_ts, group, dh), (0, 2, 1, 3, 4)).reshape(m_s, d_q)
    sb_out = jnp.concatenate([so_p, so_s.astype(BF16)], axis=0)
    x = _matmul_res([sb_out], [w_out1_b], x, "out_proj1")

    y = _moe_final(x, ln_ffn1, ln_final, w_router, w1_eb, w3_eb, w2_eb)

    return (y[:m_p].reshape(n_bp, n_tp, d), y[m_p:].reshape(n_bs, n_ts, d),
            conv_state_p, ckv[:m_p].reshape(n_bp, n_tp, kv_lora), kr_pad[:m_p, :rope].reshape(n_bp, n_tp, rope),
            h1[:m_p, d_q:d_q + n_kv * dh].reshape(n_bp, n_tp, n_kv, dh),
            h1[:m_p, d_q + n_kv * dh:].reshape(n_bp, n_tp, n_kv, dh),
            conv_state_s, ckv[m_p:].reshape(n_bs, n_ts, kv_lora), kr_pad[m_p:, :rope].reshape(n_bs, n_ts, rope),
            k_s.reshape(n_bs, n_ts, n_kv, dh), v_s.reshape(n_bs, n_ts, n_kv, dh))
```

```python
import functools

import jax
import jax.numpy as jnp
from jax import lax
from jax.experimental import pallas as pl
from jax.experimental.pallas import tpu as pltpu

F32 = jnp.float32
BF16 = jnp.bfloat16
EPS = 1e-6
ROPE_THETA = 10000.0
TOP_K = 2

LANES = 128
MXU_DIM = 256
VMEM_LIMIT = 56 * 1024 * 1024


def _tile(n, pref, align):
    t = min(pref, n) // align * align
    while t >= align:
        if n % t == 0:
            return t
        t -= align
    return n


def _params(*sem):
    return pltpu.CompilerParams(dimension_semantics=sem, vmem_limit_bytes=VMEM_LIMIT)


def _rms(x, g):
    return x * lax.rsqrt(jnp.mean(x * x, axis=-1, keepdims=True) + EPS) * g


def _dot(a, b):
    return jnp.dot(a, b, preferred_element_type=F32)


def _dot_nt(a, b):
    return lax.dot_general(a, b, (((1,), (1,)), ((), ())), preferred_element_type=F32)


def _norm_matmul_kernel(x_ref, g_ref, w_ref, o_ref, xn_ref):
    @pl.when(pl.program_id(1) == 0)
    def _():
        xn_ref[...] = _rms(x_ref[...], g_ref[...]).astype(BF16)

    o_ref[...] = _dot(xn_ref[...], w_ref[...]).astype(o_ref.dtype)


def _norm_matmul(x, g, w, name):
    m, k = x.shape
    n = w.shape[1]
    bm = _tile(m, 512, 8)
    bn = _tile(n, 640, LANES)
    return pl.pallas_call(
        _norm_matmul_kernel,
        grid=(m // bm, n // bn),
        in_specs=[pl.BlockSpec((bm, k), lambda i, j: (i, 0)),
                  pl.BlockSpec((1, k), lambda i, j: (0, 0)),
                  pl.BlockSpec((k, bn), lambda i, j: (0, j))],
        out_specs=pl.BlockSpec((bm, bn), lambda i, j: (i, j)),
        out_shape=jax.ShapeDtypeStruct((m, n), F32),
        scratch_shapes=[pltpu.VMEM((bm, k), BF16)],
        compiler_params=_params("parallel", "arbitrary"),
        name=name,
    )(x, g.reshape(1, k), w)


def _matmul_res_kernel(*refs, n_a):
    a_refs, w_refs = refs[:n_a], refs[n_a:2 * n_a]
    res_ref, o_ref = refs[2 * n_a], refs[2 * n_a + 1]
    acc = res_ref[...]
    for a_ref, w_ref in zip(a_refs, w_refs):
        acc = acc + _dot(a_ref[...], w_ref[...])
    o_ref[...] = acc


def _matmul_res(a_list, w_list, res, name):
    m, n = res.shape
    bm = _tile(m, 512, 8)
    bn = _tile(n, 1024, LANES)
    n_a = len(a_list)
    in_specs = ([pl.BlockSpec((bm, a.shape[1]), lambda i, j: (i, 0)) for a in a_list]
                + [pl.BlockSpec((w.shape[0], bn), lambda i, j: (0, j)) for w in w_list]
                + [pl.BlockSpec((bm, bn), lambda i, j: (i, j))])
    return pl.pallas_call(
        functools.partial(_matmul_res_kernel, n_a=n_a),
        grid=(m // bm, n // bn),
        in_specs=in_specs,
        out_specs=pl.BlockSpec((bm, bn), lambda i, j: (i, j)),
        out_shape=jax.ShapeDtypeStruct((m, n), F32),
        compiler_params=_params("parallel", "arbitrary"),
        name=name,
    )(*a_list, *w_list, res)


def _swiglu_tile(xn_ref, w1_ref, w3_ref, w2_ref):
    xn = xn_ref[...]
    h1 = _dot(xn, w1_ref[...])
    h3 = _dot(xn, w3_ref[...])
    h = (h1 * jax.nn.sigmoid(h1) * h3).astype(BF16)
    return _dot(h, w2_ref[...])


def _ffn_dense_kernel(x_ref, g_ref, w1_ref, w3_ref, w2_ref, o_ref, xn_ref):
    f = pl.program_id(1)

    @pl.when(f == 0)
    def _():
        xn_ref[...] = _rms(x_ref[...], g_ref[...]).astype(BF16)

    y = _swiglu_tile(xn_ref, w1_ref, w3_ref, w2_ref)

    @pl.when(f == 0)
    def _():
        o_ref[...] = x_ref[...] + y

    @pl.when(f > 0)
    def _():
        o_ref[...] += y


def _ffn_dense(x, g, w1, w3, w2, name):
    m, d = x.shape
    ff = w1.shape[1]
    bm = _tile(m, 512, 8)
    bf = _tile(ff, 512, LANES)
    return pl.pallas_call(
        _ffn_dense_kernel,
        grid=(m // bm, ff // bf),
        in_specs=[pl.BlockSpec((bm, d), lambda i, f: (i, 0)),
                  pl.BlockSpec((1, d), lambda i, f: (0, 0)),
                  pl.BlockSpec((d, bf), lambda i, f: (0, f)),
                  pl.BlockSpec((d, bf), lambda i, f: (0, f)),
                  pl.BlockSpec((bf, d), lambda i, f: (f, 0))],
        out_specs=pl.BlockSpec((bm, d), lambda i, f: (i, 0)),
        out_shape=jax.ShapeDtypeStruct((m, d), F32),
        scratch_shapes=[pltpu.VMEM((bm, d), BF16)],
        compiler_params=_params("parallel", "arbitrary"),
        name=name,
    )(x, g.reshape(1, d), w1, w3, w2)


def _row_copy(src_hbm, row, dst, slot, sem):
    return pltpu.make_async_copy(src_hbm.at[pl.ds(row, 1), :], dst.at[pl.ds(slot, 1), :], sem)


def _ffn_expert_kernel(te_ref, tb_ref, tf_ref, rt_ref, x_hbm, g_ref, w1_ref, w3_ref, w2_ref, o_ref,
                       xbuf, sem, xn_ref, *, bm):
    i = pl.program_id(0)
    f = pl.program_id(1)
    n_tiles = pl.num_programs(0)

    def tile_rows(t, fn):
        def body(r, carry):
            fn(_row_copy(x_hbm, rt_ref[t * bm + r], xbuf.at[t % 2], r, sem.at[t % 2]))
            return carry

        lax.fori_loop(0, bm, body, 0)

    def occupied(t):
        return tb_ref[jnp.minimum(t, n_tiles - 1)] == t

    @pl.when(jnp.logical_and(i == 0, f == 0))
    def _():
        tile_rows(i, lambda cp: cp.start())

    @pl.when(jnp.logical_and(jnp.logical_not(occupied(i)), f == 0))
    def _():
        o_ref[...] = jnp.zeros(o_ref.shape, F32)

    @pl.when(occupied(i))
    def _():
        @pl.when(f == 0)
        def _():
            tile_rows(i, lambda cp: cp.wait())
            xn_ref[...] = _rms(xbuf[i % 2], g_ref[...]).astype(BF16)

            @pl.when(occupied(i + 1))
            def _():
                tile_rows(i + 1, lambda cp: cp.start())

        y = _swiglu_tile(xn_ref, w1_ref, w3_ref, w2_ref)

        @pl.when(f == 0)
        def _():
            o_ref[...] = y

        @pl.when(f > 0)
        def _():
            o_ref[...] += y


def _ffn_experts(x, g, row_tok, w1_e, w3_e, w2_e, tile_expert, tile_block, bm, name):
    d = x.shape[1]
    n_rows = row_tok.shape[0]
    ff = w1_e.shape[2]
    bf = _tile(ff, 512, LANES)
    n_f = ff // bf
    n_tiles = n_rows // bm
    tile_f = jnp.where(tile_block == jnp.arange(n_tiles, dtype=jnp.int32), 0, 1).astype(jnp.int32)

    def f_idx(i, f, tf):
        return jnp.where(tf[i] == 0, f, n_f - 1)

    grid_spec = pltpu.PrefetchScalarGridSpec(
        num_scalar_prefetch=4,
        grid=(n_tiles, n_f),
        in_specs=[pl.BlockSpec(memory_space=pl.ANY),
                  pl.BlockSpec((1, d), lambda i, f, te, tb, tf, rt: (0, 0)),
                  pl.BlockSpec((None, d, bf), lambda i, f, te, tb, tf, rt: (te[i], 0, f_idx(i, f, tf))),
                  pl.BlockSpec((None, d, bf), lambda i, f, te, tb, tf, rt: (te[i], 0, f_idx(i, f, tf))),
                  pl.BlockSpec((None, bf, d), lambda i, f, te, tb, tf, rt: (te[i], f_idx(i, f, tf), 0))],
        out_specs=pl.BlockSpec((bm, d), lambda i, f, te, tb, tf, rt: (i, 0)),
        scratch_shapes=[pltpu.VMEM((2, bm, d), F32), pltpu.SemaphoreType.DMA((2,)), pltpu.VMEM((bm, d), BF16)],
    )
    return pl.pallas_call(
        functools.partial(_ffn_expert_kernel, bm=bm),
        grid_spec=grid_spec,
        out_shape=jax.ShapeDtypeStruct((n_rows, d), F32),
        compiler_params=_params("arbitrary", "arbitrary"),
        name=name,
    )(tile_expert, tile_block, tile_f, row_tok, x, g.reshape(1, d), w1_e, w3_e, w2_e)


def _ln_silu(h, g, b):
    mu = jnp.mean(h, axis=-1, keepdims=True)
    hc = h - mu
    y = hc * lax.rsqrt(jnp.mean(hc * hc, axis=-1, keepdims=True) + EPS) * g + b
    return y * jax.nn.sigmoid(y)


def _conv_prompt_kernel(a_ref, gate_ref, w_ref, cb_ref, lg_ref, lb_ref, o_ref, st_ref, uext_ref, h_ref,
                        *, bt, width, halo, rc):
    t = pl.program_id(1)
    c = a_ref.shape[1]

    @pl.when(t == 0)
    def _():
        uext_ref[0:halo, :] = jnp.zeros((halo, c), F32)

    @pl.when(t > 0)
    def _():
        uext_ref[0:halo, :] = uext_ref[bt:bt + halo, :]

    uext_ref[halo:halo + bt, :] = a_ref[...] * jax.nn.sigmoid(gate_ref[...])

    off = halo - (width - 1)
    for r0 in range(0, bt, rc):
        for c0 in range(0, c, LANES):
            acc = jnp.zeros((rc, LANES), F32)
            for k in range(width):
                acc = acc + w_ref[k:k + 1, c0:c0 + LANES] * uext_ref[r0 + off + k:r0 + off + k + rc, c0:c0 + LANES]
            h_ref[r0:r0 + rc, c0:c0 + LANES] = acc
    o_ref[...] = _ln_silu(h_ref[...] + cb_ref[...], lg_ref[...], lb_ref[...]).astype(o_ref.dtype)

    @pl.when(t == pl.num_programs(1) - 1)
    def _():
        st_ref[...] = uext_ref[bt:bt + halo, :]


def _conv_prompt(h0, n_b, n_t, c, conv_w, conv_b, ln_g, ln_b):
    width = conv_w.shape[0]
    halo = -(-(width - 1) // 8) * 8
    bt = _tile(n_t, 256, 8)
    nt = n_t // bt
    out, st = pl.pallas_call(
        functools.partial(_conv_prompt_kernel, bt=bt, width=width, halo=halo, rc=_tile(bt, 64, 8)),
        grid=(n_b, nt),
        in_specs=[pl.BlockSpec((bt, c), lambda b, t: (b * nt + t, 0)),
                  pl.BlockSpec((bt, c), lambda b, t: (b * nt + t, 1)),
                  pl.BlockSpec((width, c), lambda b, t: (0, 0)),
                  pl.BlockSpec((1, c), lambda b, t: (0, 0)),
                  pl.BlockSpec((1, c), lambda b, t: (0, 0)),
                  pl.BlockSpec((1, c), lambda b, t: (0, 0))],
        out_specs=[pl.BlockSpec((bt, c), lambda b, t: (b * nt + t, 0)),
                   pl.BlockSpec((None, halo, c), lambda b, t: (b, 0, 0))],
        out_shape=[jax.ShapeDtypeStruct((n_b * n_t, c), BF16),
                   jax.ShapeDtypeStruct((n_b, halo, c), F32)],
        scratch_shapes=[pltpu.VMEM((bt + halo, c), F32), pltpu.VMEM((bt, c), F32)],
        compiler_params=_params("parallel", "arbitrary"),
        name="conv_prompt",
    )(h0, h0, conv_w, conv_b.reshape(1, c), ln_g.reshape(1, c), ln_b.reshape(1, c))
    return out, st


def _conv_sample_kernel(a_ref, gate_ref, st_ref, wst_ref, wnew_ref, cb_ref, lg_ref, lb_ref, o_ref, u_ref, h_ref,
                        *, bb, n_t):
    u = a_ref[...] * jax.nn.sigmoid(gate_ref[...])
    u_ref[...] = u
    for b in range(bb):
        past = st_ref[b]
        new = u[b * n_t:(b + 1) * n_t, :]
        for t in range(n_t):
            h_ref[b * n_t + t:b * n_t + t + 1, :] = (
                jnp.sum(past * wst_ref[t], axis=0, keepdims=True)
                + jnp.sum(new * wnew_ref[t], axis=0, keepdims=True))
    o_ref[...] = _ln_silu(h_ref[...] + cb_ref[...], lg_ref[...], lb_ref[...]).astype(o_ref.dtype)


def _conv_sample(h0, row0, n_b, n_t, c, state, conv_w, conv_b, ln_g, ln_b):
    width = conv_w.shape[0]
    n_st = width - 1
    j = jnp.arange(n_st + n_t)[None, :] - jnp.arange(n_t)[:, None]
    w_shift = jnp.where(((j >= 0) & (j < width))[..., None], conv_w[jnp.clip(j, 0, width - 1)], 0.0)
    bb = _tile(n_b, 8, 1)
    rows = bb * n_t
    blk0 = row0 // rows
    assert row0 % rows == 0 and rows % 8 == 0
    out, u = pl.pallas_call(
        functools.partial(_conv_sample_kernel, bb=bb, n_t=n_t),
        grid=(n_b // bb,),
        in_specs=[pl.BlockSpec((rows, c), lambda i: (blk0 + i, 0)),
                  pl.BlockSpec((rows, c), lambda i: (blk0 + i, 1)),
                  pl.BlockSpec((bb, n_st, c), lambda i: (i, 0, 0)),
                  pl.BlockSpec((n_t, n_st, c), lambda i: (0, 0, 0)),
                  pl.BlockSpec((n_t, n_t, c), lambda i: (0, 0, 0)),
                  pl.BlockSpec((1, c), lambda i: (0, 0)),
                  pl.BlockSpec((1, c), lambda i: (0, 0)),
                  pl.BlockSpec((1, c), lambda i: (0, 0))],
        out_specs=[pl.BlockSpec((rows, c), lambda i: (i, 0)),
                   pl.BlockSpec((rows, c), lambda i: (i, 0))],
        out_shape=[jax.ShapeDtypeStruct((n_b * n_t, c), BF16),
                   jax.ShapeDtypeStruct((n_b * n_t, c), F32)],
        scratch_shapes=[pltpu.VMEM((rows, c), F32)],
        compiler_params=_params("parallel"),
        name="conv_sample",
    )(h0, h0, state, w_shift[:, :n_st], w_shift[:, n_st:], conv_b.reshape(1, c), ln_g.reshape(1, c),
      ln_b.reshape(1, c))
    return out, u


def _rope_lanes(x, cs, sn, half):
    lane = lax.broadcasted_iota(jnp.int32, x.shape, 1)
    swapped = jnp.where(lane % (2 * half) < half, pltpu.roll(x, LANES - half, 1), pltpu.roll(x, half, 1))
    return x * cs + swapped * sn


def _mla_prep_kernel(cq_ref, ckv_ref, kr_ref, gq_ref, gkv_ref, wuq_ref, cs_ref, sn_ref,
                     q_ref, ckv_o_ref, kr_o_ref, *, n_heads, half, scale):
    cs, sn = cs_ref[...], sn_ref[...]
    q = _dot(_rms(cq_ref[...], gq_ref[...]).astype(BF16), wuq_ref[...])
    hw = 2 * LANES
    for h in range(n_heads):
        q_ref[:, h * hw:h * hw + LANES] = (q[:, h * hw:h * hw + LANES] * scale).astype(BF16)
        q_ref[:, h * hw + LANES:(h + 1) * hw] = (
            _rope_lanes(q[:, h * hw + LANES:(h + 1) * hw], cs, sn, half) * scale).astype(BF16)
    ckv_o_ref[...] = _rms(ckv_ref[...], gkv_ref[...])
    kr_o_ref[...] = _rope_lanes(kr_ref[...], cs, sn, half)


def _mla_prep(h0, col_q, q_lora, kv_lora, g_q, g_kv, wuq_pad, cos_t, sin_t, n_heads, half, scale):
    m = h0.shape[0]
    bm = _tile(m, 512, 8)
    assert col_q % q_lora == 0 and (col_q + q_lora) % kv_lora == 0 and (col_q + q_lora + kv_lora) % LANES == 0
    hw = 2 * LANES
    return pl.pallas_call(
        functools.partial(_mla_prep_kernel, n_heads=n_heads, half=half, scale=scale),
        grid=(m // bm,),
        in_specs=[pl.BlockSpec((bm, q_lora), lambda i: (i, col_q // q_lora)),
                  pl.BlockSpec((bm, kv_lora), lambda i: (i, (col_q + q_lora) // kv_lora)),
                  pl.BlockSpec((bm, LANES), lambda i: (i, (col_q + q_lora + kv_lora) // LANES)),
                  pl.BlockSpec((1, q_lora), lambda i: (0, 0)),
                  pl.BlockSpec((1, kv_lora), lambda i: (0, 0)),
                  pl.BlockSpec((q_lora, n_heads * hw), lambda i: (0, 0)),
                  pl.BlockSpec((bm, LANES), lambda i: (i, 0)),
                  pl.BlockSpec((bm, LANES), lambda i: (i, 0))],
        out_specs=[pl.BlockSpec((bm, n_heads * hw), lambda i: (i, 0)),
                   pl.BlockSpec((bm, kv_lora), lambda i: (i, 0)),
                   pl.BlockSpec((bm, LANES), lambda i: (i, 0))],
        out_shape=[jax.ShapeDtypeStruct((m, n_heads * hw), BF16),
                   jax.ShapeDtypeStruct((m, kv_lora), F32),
                   jax.ShapeDtypeStruct((m, LANES), F32)],
        compiler_params=_params("parallel"),
        name="mla_prep",
    )(h0, h0, h0, g_q.reshape(1, -1), g_kv.reshape(1, -1), wuq_pad, cos_t, sin_t)


def _mla_kv_kernel(ckv_ref, kr_ref, wuk_ref, wuv_ref, k_ref, v_ref, *, n_heads):
    c = ckv_ref[...].astype(BF16)
    kn = _dot(c, wuk_ref[...])
    v_ref[...] = _dot(c, wuv_ref[...]).astype(BF16)
    kr = kr_ref[...].astype(BF16)
    hw = 2 * LANES
    for h in range(n_heads):
        k_ref[:, h * hw:h * hw + LANES] = kn[:, h * LANES:(h + 1) * LANES].astype(BF16)
        k_ref[:, h * hw + LANES:(h + 1) * hw] = kr


def _mla_kv(ckv, kr_pad, m_p, wuk_flat, wuv_flat, n_heads):
    kv_lora = ckv.shape[1]
    bm = _tile(m_p, 512, 8)
    hw = 2 * LANES
    return pl.pallas_call(
        functools.partial(_mla_kv_kernel, n_heads=n_heads),
        grid=(m_p // bm,),
        in_specs=[pl.BlockSpec((bm, kv_lora), lambda i: (i, 0)),
                  pl.BlockSpec((bm, LANES), lambda i: (i, 0)),
                  pl.BlockSpec(wuk_flat.shape, lambda i: (0, 0)),
                  pl.BlockSpec(wuv_flat.shape, lambda i: (0, 0))],
        out_specs=[pl.BlockSpec((bm, n_heads * hw), lambda i: (i, 0)),
                   pl.BlockSpec((bm, n_heads * LANES), lambda i: (i, 0))],
        out_shape=[jax.ShapeDtypeStruct((m_p, n_heads * hw), BF16),
                   jax.ShapeDtypeStruct((m_p, n_heads * LANES), BF16)],
        compiler_params=_params("parallel"),
        name="mla_kv_expand",
    )(ckv, kr_pad, wuk_flat, wuv_flat)


def _head_matmul_kernel(a_ref, w_ref, o_ref):
    o_ref[...] = _dot(a_ref[...].astype(BF16), w_ref[...]).astype(o_ref.dtype)


def _head_matmul(a, row0, m, col0, col_stride, w, out_dtype, name):
    n_heads, k, n = w.shape
    assert col0 % k == 0 and col_stride % k == 0 and row0 % m == 0
    return pl.pallas_call(
        _head_matmul_kernel,
        grid=(n_heads,),
        in_specs=[pl.BlockSpec((m, k), lambda h: (row0 // m, (col0 + h * col_stride) // k)),
                  pl.BlockSpec((None, k, n), lambda h: (h, 0, 0))],
        out_specs=pl.BlockSpec((m, n), lambda h: (0, h)),
        out_shape=jax.ShapeDtypeStruct((m, n_heads * n), out_dtype),
        compiler_params=_params("parallel"),
        name=name,
    )(a, w)


def _mla_flash_kernel(q_ref, k_ref, v_ref, o_ref, m_ref, l_ref, acc_ref, *, blk, hp):
    qi = pl.program_id(2)
    hw = 2 * LANES
    m_ref[...] = jnp.full(m_ref.shape, -jnp.inf, F32)
    l_ref[...] = jnp.zeros(l_ref.shape, F32)
    acc_ref[...] = jnp.zeros(acc_ref.shape, F32)

    def step(kb, masked):
        r0 = pl.multiple_of(kb * blk, blk)
        for j in range(hp):
            s = _dot_nt(q_ref[:, j * hw:(j + 1) * hw], k_ref[pl.ds(r0, blk), j * hw:(j + 1) * hw])
            if masked:
                row = lax.broadcasted_iota(jnp.int32, s.shape, 0)
                col = lax.broadcasted_iota(jnp.int32, s.shape, 1)
                s = jnp.where(col <= row, s, -jnp.inf)
            m_old = m_ref[j]
            m_new = jnp.maximum(m_old, jnp.max(s, axis=-1, keepdims=True))
            p = jnp.exp(s - m_new)
            corr = jnp.exp(m_old - m_new)
            l_ref[j] = l_ref[j] * corr + jnp.sum(p, axis=-1, keepdims=True)
            acc_ref[j] = acc_ref[j] * corr + _dot(p.astype(BF16), v_ref[pl.ds(r0, blk), j * LANES:(j + 1) * LANES])
            m_ref[j] = m_new

    step(qi, True)

    def body(kb, carry):
        step(kb, False)
        return carry

    lax.fori_loop(0, qi, body, 0)
    for j in range(hp):
        o_ref[:, j * LANES:(j + 1) * LANES] = (acc_ref[j] / l_ref[j]).astype(o_ref.dtype)


def _mla_flash(q_pad, k_pad, v, n_b, n_t, n_heads):
    blk = _tile(n_t, 512, 8)
    nq = n_t // blk
    hp = _tile(n_heads, 2, 1)
    hw = 2 * LANES
    return pl.pallas_call(
        functools.partial(_mla_flash_kernel, blk=blk, hp=hp),
        grid=(n_b, n_heads // hp, nq),
        in_specs=[pl.BlockSpec((blk, hp * hw), lambda b, h, i: (b * nq + i, h)),
                  pl.BlockSpec((n_t, hp * hw), lambda b, h, i: (b, h)),
                  pl.BlockSpec((n_t, hp * LANES), lambda b, h, i: (b, h))],
        out_specs=pl.BlockSpec((blk, hp * LANES), lambda b, h, i: (b * nq + i, h)),
        out_shape=jax.ShapeDtypeStruct((n_b * n_t, n_heads * LANES), BF16),
        scratch_shapes=[pltpu.VMEM((hp, blk, 1), F32), pltpu.VMEM((hp, blk, 1), F32),
                        pltpu.VMEM((hp, blk, LANES), F32)],
        compiler_params=_params("parallel", "parallel", "arbitrary"),
        name="mla_prompt_attn",
    )(q_pad, k_pad, v)


def _first_slot(i):
    return i % 2


def _loop_slot(s):
    return 2 + (s - 1) % 2


def _mla_decode_kernel(pt_ref, ql_ref, qr_ref, cnew_ref, rnew_ref, ckv_hbm, kr_hbm, o_ref,
                       cbuf, rbuf, csem, rsem, m_ref, l_ref, acc_ref, *, n_seq, n_pg, n_steps, n_t, n_heads):
    i = pl.program_id(0)

    def copies(ii, step, slot):
        out = []
        for q in range(n_seq):
            for g in range(n_pg):
                pg = pt_ref[ii * n_seq + q, step * n_pg + g]
                out.append(pltpu.make_async_copy(ckv_hbm.at[pg], cbuf.at[slot, q * n_pg + g], csem.at[slot]))
                out.append(pltpu.make_async_copy(kr_hbm.at[pg], rbuf.at[slot, q * n_pg + g], rsem.at[slot]))
        return out

    def start(ii, step, slot):
        for cp in copies(ii, step, slot):
            cp.start()

    def wait(ii, step, slot):
        for cp in copies(ii, step, slot):
            cp.wait()

    @pl.when(i == 0)
    def _():
        start(i, 0, _first_slot(i))

    @pl.when(i + 1 < pl.num_programs(0))
    def _():
        start(i + 1, 0, _first_slot(i + 1))

    if n_steps > 1:
        start(i, 1, _loop_slot(1))

    rows = ql_ref.shape[1]
    t_row = lax.broadcasted_iota(jnp.int32, (rows, 1), 0) // n_heads
    for q in range(n_seq):
        qlf, qrf = ql_ref[q].astype(F32), qr_ref[q].astype(F32)
        m = jnp.full((rows, 1), -jnp.inf, F32)
        l = jnp.zeros((rows, 1), F32)
        acc = jnp.zeros(acc_ref.shape[1:], F32)
        for t in range(n_t):
            c = cnew_ref[q, t:t + 1, :]
            s = (jnp.sum(qlf * c, axis=-1, keepdims=True)
                 + jnp.sum(qrf * rnew_ref[q, t:t + 1, :], axis=-1, keepdims=True))
            s = jnp.where(t <= t_row, s, -jnp.inf)
            m_new = jnp.maximum(m, s)
            corr = jnp.exp(m - m_new)
            p = jnp.exp(s - m_new)
            l = l * corr + p
            acc = acc * corr + p * c
            m = m_new
        m_ref[q] = m
        l_ref[q] = l
        acc_ref[q] = acc

    def compute(slot):
        rope = rbuf.shape[2]
        for q in range(n_seq):
            ql = ql_ref[q]
            qr = qr_ref[q][:, :rope]
            ckv = [cbuf[slot, q * n_pg + g].astype(BF16) for g in range(n_pg)]
            s = jnp.concatenate(
                [_dot_nt(ql, c) + _dot(qr, rbuf[slot, q * n_pg + g].astype(BF16)) for g, c in enumerate(ckv)],
                axis=-1)
            m_old = m_ref[q]
            m_new = jnp.maximum(m_old, jnp.max(s, axis=-1, keepdims=True))
            p = jnp.exp(s - m_new)
            corr = jnp.exp(m_old - m_new)
            l_ref[q] = l_ref[q] * corr + jnp.sum(p, axis=-1, keepdims=True)
            acc = acc_ref[q] * corr
            pg = ckv[0].shape[0]
            for g, c in enumerate(ckv):
                acc = acc + _dot(p[:, g * pg:(g + 1) * pg].astype(BF16), c)
            acc_ref[q] = acc
            m_ref[q] = m_new

    wait(i, 0, _first_slot(i))
    compute(_first_slot(i))

    def body(s, carry):
        @pl.when(s + 1 < n_steps)
        def _():
            start(i, s + 1, _loop_slot(s + 1))

        wait(i, s, _loop_slot(s))
        compute(_loop_slot(s))
        return carry

    lax.fori_loop(1, n_steps, body, 0)
    for q in range(n_seq):
        o_ref[q] = acc_ref[q] / l_ref[q]


def _mla_decode(page_table, q_lat, q_rope, ckv_new, kr_new, cache_ckv, cache_kr_t, n_heads):
    n_b, rows, c = q_lat.shape
    n_t = ckv_new.shape[1]
    n_pages = page_table.shape[1]
    page = cache_ckv.shape[1]
    rope = cache_kr_t.shape[1]
    n_pg = _tile(n_pages, 8, 1)
    n_seq = _tile(n_b, 2, 1)
    grid_spec = pltpu.PrefetchScalarGridSpec(
        num_scalar_prefetch=1,
        grid=(n_b // n_seq,),
        in_specs=[pl.BlockSpec((n_seq, rows, c), lambda i, pt: (i, 0, 0)),
                  pl.BlockSpec((n_seq, rows, LANES), lambda i, pt: (i, 0, 0)),
                  pl.BlockSpec((n_seq, n_t, c), lambda i, pt: (i, 0, 0)),
                  pl.BlockSpec((n_seq, n_t, LANES), lambda i, pt: (i, 0, 0)),
                  pl.BlockSpec(memory_space=pl.ANY),
                  pl.BlockSpec(memory_space=pl.ANY)],
        out_specs=pl.BlockSpec((n_seq, rows, c), lambda i, pt: (i, 0, 0)),
        scratch_shapes=[pltpu.VMEM((4, n_seq * n_pg, page, c), F32), pltpu.VMEM((4, n_seq * n_pg, rope, page), F32),
                        pltpu.SemaphoreType.DMA((4,)), pltpu.SemaphoreType.DMA((4,)),
                        pltpu.VMEM((n_seq, rows, 1), F32), pltpu.VMEM((n_seq, rows, 1), F32),
                        pltpu.VMEM((n_seq, rows, c), F32)],
    )
    return pl.pallas_call(
        functools.partial(_mla_decode_kernel, n_seq=n_seq, n_pg=n_pg, n_steps=n_pages // n_pg, n_t=n_t,
                          n_heads=n_heads),
        grid_spec=grid_spec,
        out_shape=jax.ShapeDtypeStruct((n_b, rows, c), F32),
        compiler_params=_params("arbitrary"),
        name="mla_sample_attn",
    )(page_table, q_lat, q_rope, ckv_new, kr_new, cache_ckv, cache_kr_t)


SB_DEAD_TAIL = -104.0


def _sb_block(q, k, v, u_ref, tail, mask):
    z = _dot_nt(q, k)
    sp = jnp.log1p(jnp.exp(-jnp.abs(z)))
    log_beta = jnp.minimum(z, 0.0) - sp
    log_1m = jnp.minimum(-z, 0.0) - sp
    if mask is not None:
        log_1m = jnp.where(mask, log_1m, 0.0)
    hi = log_1m.astype(BF16)
    lo = (log_1m - hi.astype(F32)).astype(BF16)
    later = _dot(hi, u_ref[...]) + _dot(lo, u_ref[...]) + tail
    w = jnp.exp(log_beta + later)
    if mask is not None:
        w = jnp.where(mask, w, 0.0)
    return _dot(w.astype(BF16), v), tail + jnp.sum(log_1m, axis=-1, keepdims=True)


def _suffix_matrix(n):
    j = jnp.arange(n)
    return (j[:, None] > j[None, :]).astype(BF16)


def _alive(tail_ref):
    return (jnp.max(tail_ref[...]) > SB_DEAD_TAIL).astype(jnp.int32)


def _sb_prompt_kernel(q_ref, k_ref, v_ref, u_ref, o_ref, acc_ref, tail_ref, *, blk, hp, dh, scale):
    qi = pl.program_id(3)

    def q_head(j):
        return (q_ref[:, j * dh:(j + 1) * dh] * scale).astype(BF16)

    def kv(kb):
        r0 = pl.multiple_of(kb * blk, blk)
        return k_ref[pl.ds(r0, blk), :].astype(BF16), v_ref[pl.ds(r0, blk), :].astype(BF16)

    row = lax.broadcasted_iota(jnp.int32, (blk, blk), 0)
    col = lax.broadcasted_iota(jnp.int32, (blk, blk), 1)
    k, v = kv(qi)
    for j in range(hp):
        acc, tail = _sb_block(q_head(j), k, v, u_ref, jnp.zeros((blk, 1), F32), col < row)
        acc_ref[j] = acc
        tail_ref[j] = tail

    def cond(c):
        return jnp.logical_and(c[0] < qi, c[1] == 1)

    def body(c):
        k, v = kv(qi - 1 - c[0])
        for j in range(hp):
            acc, tail = _sb_block(q_head(j), k, v, u_ref, tail_ref[j], None)
            acc_ref[j] += acc
            tail_ref[j] = tail
        return c[0] + 1, _alive(tail_ref)

    lax.while_loop(cond, body, (jnp.int32(0), _alive(tail_ref)))
    for j in range(hp):
        o_ref[:, j * dh:(j + 1) * dh] = acc_ref[j].astype(o_ref.dtype)


def _sb_prompt(h1, n_b, n_t, n_heads, n_kv, dh, scale):
    blk = _tile(n_t, 256, 8)
    nq = n_t // blk
    group = n_heads // n_kv
    hp = _tile(group, 4, 1)
    gp = group // hp
    return pl.pallas_call(
        functools.partial(_sb_prompt_kernel, blk=blk, hp=hp, dh=dh, scale=scale),
        grid=(n_b, n_kv, gp, nq),
        in_specs=[pl.BlockSpec((blk, hp * dh), lambda b, g, r, i: (b * nq + i, g * gp + r)),
                  pl.BlockSpec((n_t, dh), lambda b, g, r, i: (b, n_heads + g)),
                  pl.BlockSpec((n_t, dh), lambda b, g, r, i: (b, n_heads + n_kv + g)),
                  pl.BlockSpec((blk, blk), lambda b, g, r, i: (0, 0))],
        out_specs=pl.BlockSpec((blk, hp * dh), lambda b, g, r, i: (b * nq + i, g * gp + r)),
        out_shape=jax.ShapeDtypeStruct((n_b * n_t, n_heads * dh), BF16),
        scratch_shapes=[pltpu.VMEM((hp, blk, dh), F32), pltpu.VMEM((hp, blk, 1), F32)],
        compiler_params=_params("parallel", "parallel", "parallel", "arbitrary"),
        name="sb_prompt_attn",
    )(h1, h1, h1, _suffix_matrix(blk))


def _sb_decode_kernel(pt_ref, q_ref, kn_ref, vn_ref, u_ref, k_hbm, v_hbm, o_ref,
                      kbuf, vbuf, ksem, vsem, acc_ref, tail_ref, alive_ref,
                      *, n_pg, n_steps, n_pages, page, n_t, n_kv, group, dh, scale):
    b = pl.program_id(0)

    def copies(bb, step, slot):
        out = []
        for g in range(n_pg):
            pg = pt_ref[bb, n_pages - 1 - (step * n_pg + g)]
            out.append(pltpu.make_async_copy(k_hbm.at[pg], kbuf.at[slot, g], ksem.at[slot]))
            out.append(pltpu.make_async_copy(v_hbm.at[pg], vbuf.at[slot, g], vsem.at[slot]))
        return out

    def start(bb, step, slot):
        for cp in copies(bb, step, slot):
            cp.start()

    def wait(bb, step, slot):
        for cp in copies(bb, step, slot):
            cp.wait()

    @pl.when(b == 0)
    def _():
        start(b, 0, _first_slot(b))

    @pl.when(b + 1 < pl.num_programs(0))
    def _():
        start(b + 1, 0, _first_slot(b + 1))

    if n_steps > 1:
        start(b, 1, _loop_slot(1))

    t_row = lax.broadcasted_iota(jnp.int32, (n_t * group, 1), 0) // group
    for g in range(n_kv):
        q = (q_ref[g] * scale).astype(BF16).astype(F32)
        tail = jnp.zeros((n_t * group, 1), F32)
        acc = jnp.zeros((n_t * group, dh), F32)
        for t in reversed(range(n_t)):
            z = jnp.sum(q * kn_ref[t:t + 1, g * dh:(g + 1) * dh], axis=-1, keepdims=True)
            sp = jnp.log1p(jnp.exp(-jnp.abs(z)))
            mask = t < t_row
            w = jnp.where(mask, jnp.exp(jnp.minimum(z, 0.0) - sp + tail), 0.0)
            acc = acc + w * vn_ref[t:t + 1, g * dh:(g + 1) * dh]
            tail = tail + jnp.where(mask, jnp.minimum(-z, 0.0) - sp, 0.0)
        acc_ref[g] = acc
        tail_ref[g] = tail
    alive_ref[0] = 1

    def compute(slot):
        for g in range(n_pg):
            @pl.when(alive_ref[0] == 1)
            def _():
                for grp in range(n_kv):
                    q = (q_ref[grp] * scale).astype(BF16)
                    d, tail = _sb_block(q, kbuf[slot, g, pl.ds(grp, page, stride=n_kv), :].astype(BF16),
                                        vbuf[slot, g, pl.ds(grp, page, stride=n_kv), :].astype(BF16),
                                        u_ref, tail_ref[grp], None)
                    acc_ref[grp] += d
                    tail_ref[grp] = tail
                alive_ref[0] = _alive(tail_ref)

    wait(b, 0, _first_slot(b))
    compute(_first_slot(b))

    if n_steps > 1:
        def cond(c):
            return jnp.logical_and(c[0] < n_steps, c[1] == 1)

        def body(c):
            s = c[0]

            @pl.when(s + 1 < n_steps)
            def _():
                start(b, s + 1, _loop_slot(s + 1))

            wait(b, s, _loop_slot(s))
            compute(_loop_slot(s))
            return s + 1, alive_ref[0]

        s_end, _ = lax.while_loop(cond, body, (jnp.int32(1), alive_ref[0]))

        @pl.when(s_end < n_steps)
        def _():
            wait(b, s_end, _loop_slot(s_end))

    o_ref[...] = acc_ref[...]


def _sb_decode(page_table, q, k_new, v_new, cache_k, cache_v, scale):
    n_b, n_kv, rows, dh = q.shape
    n_t = k_new.shape[1]
    group = rows // n_t
    n_pages = page_table.shape[1]
    page = cache_k.shape[1] // n_kv
    n_pg = _tile(n_pages, 4, 1)
    grid_spec = pltpu.PrefetchScalarGridSpec(
        num_scalar_prefetch=1,
        grid=(n_b,),
        in_specs=[pl.BlockSpec((None, n_kv, rows, dh), lambda b, pt: (b, 0, 0, 0)),
                  pl.BlockSpec((None, n_t, n_kv * dh), lambda b, pt: (b, 0, 0)),
                  pl.BlockSpec((None, n_t, n_kv * dh), lambda b, pt: (b, 0, 0)),
                  pl.BlockSpec((page, page), lambda b, pt: (0, 0)),
                  pl.BlockSpec(memory_space=pl.ANY),
                  pl.BlockSpec(memory_space=pl.ANY)],
        out_specs=pl.BlockSpec((None, n_kv, rows, dh), lambda b, pt: (b, 0, 0, 0)),
        scratch_shapes=[pltpu.VMEM((4, n_pg, page * n_kv, dh), F32), pltpu.VMEM((4, n_pg, page * n_kv, dh), F32),
                        pltpu.SemaphoreType.DMA((4,)), pltpu.SemaphoreType.DMA((4,)),
                        pltpu.VMEM((n_kv, rows, dh), F32), pltpu.VMEM((n_kv, rows, 1), F32),
                        pltpu.SMEM((1,), jnp.int32)],
    )
    return pl.pallas_call(
        functools.partial(_sb_decode_kernel, n_pg=n_pg, n_steps=n_pages // n_pg, n_pages=n_pages, page=page,
                          n_t=n_t, n_kv=n_kv, group=group, dh=dh, scale=scale),
        grid_spec=grid_spec,
        out_shape=jax.ShapeDtypeStruct((n_b, n_kv, rows, dh), F32),
        compiler_params=_params("arbitrary"),
        name="sb_sample_attn",
    )(page_table, q, k_new, v_new, _suffix_matrix(page), cache_k, cache_v)


def _router_kernel(x_ref, g_ref, wr_ref, gate_ref, sel_ref, *, n_exp):
    xn = _rms(x_ref[...], g_ref[...])
    logits = jnp.concatenate(
        [jnp.sum(xn * wr_ref[e:e + 1, :], axis=-1, keepdims=True) for e in range(n_exp)], axis=-1)
    e_id = lax.broadcasted_iota(jnp.int32, logits.shape, 1)
    m1 = jnp.max(logits, axis=-1, keepdims=True)
    i1 = jnp.min(jnp.where(logits == m1, e_id, n_exp), axis=-1, keepdims=True)
    rest = jnp.where(e_id == i1, -jnp.inf, logits)
    m2 = jnp.max(rest, axis=-1, keepdims=True)
    i2 = jnp.min(jnp.where(rest == m2, e_id, n_exp), axis=-1, keepdims=True)
    e2 = jnp.exp(m2 - m1)
    den = 1.0 + e2
    gate_ref[...] = jnp.where(e_id == i1, 1.0 / den, 0.0) + jnp.where(e_id == i2, e2 / den, 0.0)
    sel_ref[...] = jnp.where((e_id == i1) | (e_id == i2), 1.0, 0.0)


def _router(x, g, w_router):
    m, d = x.shape
    n_exp = w_router.shape[1]
    bm = _tile(m, 512, 8)
    return pl.pallas_call(
        functools.partial(_router_kernel, n_exp=n_exp),
        grid=(m // bm,),
        in_specs=[pl.BlockSpec((bm, d), lambda i: (i, 0)),
                  pl.BlockSpec((1, d), lambda i: (0, 0)),
                  pl.BlockSpec((n_exp, d), lambda i: (0, 0))],
        out_specs=[pl.BlockSpec((bm, n_exp), lambda i: (i, 0))] * 2,
        out_shape=[jax.ShapeDtypeStruct((m, n_exp), F32)] * 2,
        compiler_params=_params("parallel"),
        name="moe_router",
    )(x, g.reshape(1, d), w_router.T)


def _row_index_kernel(p0_ref, p1_ref, o_ref):
    def zero(r, carry):
        o_ref[r] = 0
        return carry

    def put(t, carry):
        o_ref[p0_ref[t]] = t
        o_ref[p1_ref[t]] = t
        return carry

    lax.fori_loop(0, o_ref.shape[0], zero, 0)
    lax.fori_loop(0, p0_ref.shape[0], put, 0)


def _row_index(pos0, pos1, n_rows):
    return pl.pallas_call(
        _row_index_kernel,
        in_specs=[pl.BlockSpec(memory_space=pltpu.SMEM)] * 2,
        out_specs=pl.BlockSpec(memory_space=pltpu.SMEM),
        out_shape=jax.ShapeDtypeStruct((n_rows,), jnp.int32),
        name="moe_row_index",
    )(pos0, pos1)


def _combine_kernel(p0_ref, p1_ref, x_ref, g_ref, g0_ref, g1_ref, y_hbm, o_ref, buf_ref, sem, *, bm):
    i = pl.program_id(0)

    def tile_rows(t, fn):
        def body(r, carry):
            fn(_row_copy(y_hbm, p0_ref[t * bm + r], buf_ref.at[t % 2, 0], r, sem.at[t % 2]))
            fn(_row_copy(y_hbm, p1_ref[t * bm + r], buf_ref.at[t % 2, 1], r, sem.at[t % 2]))
            return carry

        lax.fori_loop(0, bm, body, 0)

    @pl.when(i == 0)
    def _():
        tile_rows(i, lambda cp: cp.start())

    @pl.when(i + 1 < pl.num_programs(0))
    def _():
        tile_rows(i + 1, lambda cp: cp.start())

    tile_rows(i, lambda cp: cp.wait())
    y = g0_ref[...] * buf_ref[i % 2, 0] + g1_ref[...] * buf_ref[i % 2, 1]
    o_ref[...] = _rms(x_ref[...] + y, g_ref[...])


def _combine_norm(x, g, ys, pos0, pos1, g0, g1, bm):
    m, d = x.shape
    grid_spec = pltpu.PrefetchScalarGridSpec(
        num_scalar_prefetch=2,
        grid=(m // bm,),
        in_specs=[pl.BlockSpec((bm, d), lambda i, p0, p1: (i, 0)),
                  pl.BlockSpec((1, d), lambda i, p0, p1: (0, 0)),
                  pl.BlockSpec((bm, 1), lambda i, p0, p1: (i, 0)),
                  pl.BlockSpec((bm, 1), lambda i, p0, p1: (i, 0)),
                  pl.BlockSpec(memory_space=pl.ANY)],
        out_specs=pl.BlockSpec((bm, d), lambda i, p0, p1: (i, 0)),
        scratch_shapes=[pltpu.VMEM((2, 2, bm, d), F32), pltpu.SemaphoreType.DMA((2,))],
    )
    return pl.pallas_call(
        functools.partial(_combine_kernel, bm=bm),
        grid_spec=grid_spec,
        out_shape=jax.ShapeDtypeStruct((m, d), F32),
        compiler_params=_params("arbitrary"),
        name="moe_combine_norm",
    )(pos0, pos1, x, g.reshape(1, d), g0, g1, ys)


def _moe_final(x, g_ffn, g_final, w_router, w1_e, w3_e, w2_e):
    m, d = x.shape
    n_exp = w_router.shape[1]
    bm = _tile(m, 512, 8)
    gate, sel = _router(x, g_ffn, w_router)
    sel = sel > 0.0
    cnt = jnp.cumsum(sel.astype(jnp.int32), axis=0)
    n_e = cnt[-1]
    padded = (n_e + bm - 1) // bm * bm
    ends = jnp.cumsum(padded)
    dest = (ends - padded)[None, :] + cnt - 1
    n_rows = (TOP_K * m + n_exp * (bm - 1)) // bm * bm
    n_tiles = n_rows // bm
    n_used = jnp.maximum(ends[-1] // bm, 1)
    tile_block = jnp.minimum(jnp.arange(n_tiles, dtype=jnp.int32), n_used - 1)
    tile_expert = jnp.minimum(jnp.sum(tile_block[:, None] * bm >= ends[None, :], axis=1), n_exp - 1)
    first = jnp.argmax(sel, axis=1)[:, None]
    last = (n_exp - 1 - jnp.argmax(sel[:, ::-1], axis=1))[:, None]
    pos0 = jnp.take_along_axis(dest, first, axis=1)[:, 0].astype(jnp.int32)
    pos1 = jnp.take_along_axis(dest, last, axis=1)[:, 0].astype(jnp.int32)
    g0 = jnp.take_along_axis(gate, first, axis=1)
    g1 = jnp.take_along_axis(gate, last, axis=1)

    row_tok = _row_index(pos0, pos1, n_rows)
    ys = _ffn_experts(x, g_ffn, row_tok, w1_e, w3_e, w2_e, tile_expert.astype(jnp.int32), tile_block, bm,
                      "moe_experts")
    return _combine_norm(x, g_final, ys, pos0, pos1, g0, g1, _tile(m, 256, 8))


def kernel(x_prompt, x_sample, state_conv0, cache_mla_ckv, cache_mla_krope, cache_sb_k, cache_sb_v, page_table,
           ln_mix0, w_in0, conv_w, conv_b, conv_ln_g, conv_ln_b, q_norm_g, kv_norm_g, w_uq, w_uk, w_uv, w_out0,
           ln_ffn0, w1_d, w3_d, w2_d, ln_mix1, w_in1, w_out1, ln_ffn1, w_router, w1_e, w3_e, w2_e, ln_final):
    n_bp, n_tp, d = x_prompt.shape
    n_bs, n_ts, _ = x_sample.shape
    m_p, m_s = n_bp * n_tp, n_bs * n_ts
    c_conv = conv_w.shape[1]
    q_lora, n_hm, qk_dim = w_uq.shape
    kv_lora, _, nope = w_uk.shape
    v_dim = w_uv.shape[2]
    rope = qk_dim - nope
    half = rope // 2
    page = cache_mla_ckv.shape[1]
    past_len = page_table.shape[1] * page
    n_kv, dh = cache_sb_k.shape[2], cache_sb_k.shape[3]
    n_hs = w_out1.shape[0] // dh
    group = n_hs // n_kv
    assert nope == LANES and v_dim == LANES and dh == LANES and rope <= LANES

    x = jnp.concatenate([x_prompt.reshape(m_p, d), x_sample.reshape(m_s, d)], axis=0)

    d_in0 = w_in0.shape[1]
    d_in0_pad = -(-d_in0 // 640) * 640 if d_in0 > 640 else -(-d_in0 // LANES) * LANES
    w_in0_p = jnp.pad(w_in0, ((0, 0), (0, d_in0_pad - d_in0))).astype(BF16)
    hw = 2 * LANES
    wuq_pad = jnp.pad(w_uq, ((0, 0), (0, 0), (0, hw - qk_dim))).reshape(q_lora, n_hm * hw).astype(BF16)
    wuk_flat = w_uk.reshape(kv_lora, n_hm * nope).astype(BF16)
    wuv_flat = w_uv.reshape(kv_lora, n_hm * v_dim).astype(BF16)
    wuk_t = jnp.transpose(w_uk, (1, 2, 0)).astype(BF16)
    wuv_h = jnp.transpose(w_uv, (1, 0, 2)).astype(BF16)
    w_out0_b = w_out0.astype(BF16)
    w1_db, w3_db, w2_db = w1_d.astype(BF16), w3_d.astype(BF16), w2_d.astype(BF16)
    w_in1_b, w_out1_b = w_in1.astype(BF16), w_out1.astype(BF16)
    w1_eb, w3_eb, w2_eb = w1_e.astype(BF16), w3_e.astype(BF16), w2_e.astype(BF16)

    pos = jnp.concatenate([jnp.tile(jnp.arange(n_tp, dtype=F32), n_bp),
                           jnp.tile(past_len + jnp.arange(n_ts, dtype=F32), n_bs)])
    freqs = ROPE_THETA ** (-jnp.arange(half, dtype=F32) / half)
    ang = pos[:, None] * freqs[None, :]
    zpad = jnp.zeros((m_p + m_s, LANES - rope), F32)
    cos_t = jnp.concatenate([jnp.cos(ang), jnp.cos(ang), zpad], axis=1)
    sin_t = jnp.concatenate([-jnp.sin(ang), jnp.sin(ang), zpad], axis=1)

    h0 = _norm_matmul(x, ln_mix0, w_in0_p, "in_proj0")
    conv_p, st_p = _conv_prompt(h0, n_bp, n_tp, c_conv, conv_w, conv_b, conv_ln_g, conv_ln_b)
    conv_s, u_s = _conv_sample(h0, m_p, n_bs, n_ts, c_conv, state_conv0, conv_w, conv_b, conv_ln_g, conv_ln_b)
    n_state = conv_w.shape[0] - 1
    conv_state_p = st_p[:, st_p.shape[1] - n_state:, :]
    conv_state_s = jnp.concatenate([state_conv0, u_s.reshape(n_bs, n_ts, c_conv)], axis=1)[:, -n_state:, :]

    mla_scale = float(qk_dim) ** -0.5
    q_pad, ckv, kr_pad = _mla_prep(h0, 2 * c_conv, q_lora, kv_lora, q_norm_g, kv_norm_g, wuq_pad, cos_t, sin_t,
                                   n_hm, half, mla_scale)
    k_pad, v_p = _mla_kv(ckv, kr_pad, m_p, wuk_flat, wuv_flat, n_hm)
    o_p = _mla_flash(q_pad, k_pad, v_p, n_bp, n_tp, n_hm)

    q_lat = _head_matmul(q_pad, m_p, m_s, 0, hw, wuk_t, BF16, "mla_q_absorb")
    q_lat = q_lat.reshape(n_bs, n_ts * n_hm, kv_lora)
    q_rope_s = q_pad[m_p:].reshape(n_bs, n_ts * n_hm, hw)[:, :, LANES:]
    o_lat = _mla_decode(page_table, q_lat, q_rope_s, ckv[m_p:].reshape(n_bs, n_ts, kv_lora),
                        kr_pad[m_p:].reshape(n_bs, n_ts, LANES), cache_mla_ckv,
                        jnp.transpose(cache_mla_krope, (0, 2, 1)), n_hm)
    o_s = _head_matmul(o_lat.reshape(m_s, n_hm * kv_lora), 0, m_s, 0, kv_lora, wuv_h, BF16, "mla_v_absorb")

    conv_out = jnp.concatenate([conv_p, conv_s], axis=0)
    mla_out = jnp.concatenate([o_p, o_s], axis=0)
    x = _matmul_res([conv_out, mla_out], [w_out0_b[:c_conv], w_out0_b[c_conv:]], x, "out_proj0")
    x = _ffn_dense(x, ln_ffn0, w1_db, w3_db, w2_db, "ffn0")

    h1 = _norm_matmul(x, ln_mix1, w_in1_b, "in_proj1")
    sb_scale = float(dh) ** -0.5
    so_p = _sb_prompt(h1, n_bp, n_tp, n_hs, n_kv, dh, sb_scale)
    d_q = n_hs * dh
    h1_s = h1[m_p:]
    q_s = jnp.transpose(h1_s[:, :d_q].reshape(n_bs, n_ts, n_kv, group, dh), (0, 2, 1, 3, 4))
    q_s = q_s.reshape(n_bs, n_kv, n_ts * group, dh)
    k_s = h1_s[:, d_q:d_q + n_kv * dh].reshape(n_bs, n_ts, n_kv * dh)
    v_s = h1_s[:, d_q + n_kv * dh:].reshape(n_bs, n_ts, n_kv * dh)
    so_s = _sb_decode(page_table, q_s, k_s, v_s, cache_sb_k.reshape(-1, page * n_kv, dh),
                      cache_sb_v.reshape(-1, page * n_kv, dh), sb_scale)
    so_s = jnp.transpose(so_s.reshape(n_bs, n_kv, n_ts, group, dh), (0, 2, 1, 3, 4)).reshape(m_s, d_q)
    sb_out = jnp.concatenate([so_p, so_s.astype(BF16)], axis=0)
    x = _matmul_res([sb_out], [w_out1_b], x, "out_proj1")

    y = _moe_final(x, ln_ffn1, ln_final, w_router, w1_eb, w3_eb, w2_eb)

    return (y[:m_p].reshape(n_bp, n_tp, d), y[m_p:].reshape(n_bs, n_ts, d),
            conv_state_p, ckv[:m_p].reshape(n_bp, n_tp, kv_lora), kr_pad[:m_p, :rope].reshape(n_bp, n_tp, rope),
            h1[:m_p, d_q:d_q + n_kv * dh].reshape(n_bp, n_tp, n_kv, dh),
            h1[:m_p, d_q + n_kv * dh:].reshape(n_bp, n_tp, n_kv, dh),
            conv_state_s, ckv[m_p:].reshape(n_bs, n_ts, kv_lora), kr_pad[m_p:, :rope].reshape(n_bs, n_ts, rope),
            k_s.reshape(n_bs, n_ts, n_kv, dh), v_s.reshape(n_bs, n_ts, n_kv, dh))
```

```python
import functools

import jax
import jax.numpy as jnp
from jax import lax
from jax.experimental import pallas as pl
from jax.experimental.pallas import tpu as pltpu

F32 = jnp.float32
BF16 = jnp.bfloat16
EPS = 1e-6
ROPE_THETA = 10000.0
TOP_K = 2

LANES = 128
MXU_DIM = 256
VMEM_LIMIT = 56 * 1024 * 1024


def _tile(n, pref, align):
    t = min(pref, n) // align * align
    while t >= align:
        if n % t == 0:
            return t
        t -= align
    return n


def _params(*sem):
    return pltpu.CompilerParams(dimension_semantics=sem, vmem_limit_bytes=VMEM_LIMIT)


def _rms(x, g):
    return x * lax.rsqrt(jnp.mean(x * x, axis=-1, keepdims=True) + EPS) * g


def _dot(a, b):
    return jnp.dot(a, b, preferred_element_type=F32)


def _dot_nt(a, b):
    return lax.dot_general(a, b, (((1,), (1,)), ((), ())), preferred_element_type=F32)


def _norm_matmul_kernel(x_ref, g_ref, w_ref, o_ref, xn_ref):
    @pl.when(pl.program_id(1) == 0)
    def _():
        xn_ref[...] = _rms(x_ref[...], g_ref[...]).astype(BF16)

    o_ref[...] = _dot(xn_ref[...], w_ref[...]).astype(o_ref.dtype)


def _norm_matmul(x, g, w, name):
    m, k = x.shape
    n = w.shape[1]
    bm = _tile(m, 512, 8)
    bn = _tile(n, 640, LANES)
    return pl.pallas_call(
        _norm_matmul_kernel,
        grid=(m // bm, n // bn),
        in_specs=[pl.BlockSpec((bm, k), lambda i, j: (i, 0)),
                  pl.BlockSpec((1, k), lambda i, j: (0, 0)),
                  pl.BlockSpec((k, bn), lambda i, j: (0, j))],
        out_specs=pl.BlockSpec((bm, bn), lambda i, j: (i, j)),
        out_shape=jax.ShapeDtypeStruct((m, n), F32),
        scratch_shapes=[pltpu.VMEM((bm, k), BF16)],
        compiler_params=_params("parallel", "arbitrary"),
        name=name,
    )(x, g.reshape(1, k), w)


def _matmul_res_kernel(*refs, n_a):
    a_refs, w_refs = refs[:n_a], refs[n_a:2 * n_a]
    res_ref, o_ref = refs[2 * n_a], refs[2 * n_a + 1]
    acc = res_ref[...]
    for a_ref, w_ref in zip(a_refs, w_refs):
        acc = acc + _dot(a_ref[...], w_ref[...])
    o_ref[...] = acc


def _matmul_res(a_list, w_list, res, name):
    m, n = res.shape
    bm = _tile(m, 512, 8)
    bn = _tile(n, 1024, LANES)
    n_a = len(a_list)
    in_specs = ([pl.BlockSpec((bm, a.shape[1]), lambda i, j: (i, 0)) for a in a_list]
                + [pl.BlockSpec((w.shape[0], bn), lambda i, j: (0, j)) for w in w_list]
                + [pl.BlockSpec((bm, bn), lambda i, j: (i, j))])
    return pl.pallas_call(
        functools.partial(_matmul_res_kernel, n_a=n_a),
        grid=(m // bm, n // bn),
        in_specs=in_specs,
        out_specs=pl.BlockSpec((bm, bn), lambda i, j: (i, j)),
        out_shape=jax.ShapeDtypeStruct((m, n), F32),
        compiler_params=_params("parallel", "arbitrary"),
        name=name,
    )(*a_list, *w_list, res)


def _swiglu_tile(xn_ref, w1_ref, w3_ref, w2_ref):
    xn = xn_ref[...]
    h1 = _dot(xn, w1_ref[...])
    h3 = _dot(xn, w3_ref[...])
    h = (h1 * jax.nn.sigmoid(h1) * h3).astype(BF16)
    return _dot(h, w2_ref[...])


def _ffn_dense_kernel(x_ref, g_ref, w1_ref, w3_ref, w2_ref, o_ref, xn_ref):
    f = pl.program_id(1)

    @pl.when(f == 0)
    def _():
        xn_ref[...] = _rms(x_ref[...], g_ref[...]).astype(BF16)

    y = _swiglu_tile(xn_ref, w1_ref, w3_ref, w2_ref)

    @pl.when(f == 0)
    def _():
        o_ref[...] = x_ref[...] + y

    @pl.when(f > 0)
    def _():
        o_ref[...] += y


def _ffn_dense(x, g, w1, w3, w2, name):
    m, d = x.shape
    ff = w1.shape[1]
    bm = _tile(m, 512, 8)
    bf = _tile(ff, 512, LANES)
    return pl.pallas_call(
        _ffn_dense_kernel,
        grid=(m // bm, ff // bf),
        in_specs=[pl.BlockSpec((bm, d), lambda i, f: (i, 0)),
                  pl.BlockSpec((1, d), lambda i, f: (0, 0)),
                  pl.BlockSpec((d, bf), lambda i, f: (0, f)),
                  pl.BlockSpec((d, bf), lambda i, f: (0, f)),
                  pl.BlockSpec((bf, d), lambda i, f: (f, 0))],
        out_specs=pl.BlockSpec((bm, d), lambda i, f: (i, 0)),
        out_shape=jax.ShapeDtypeStruct((m, d), F32),
        scratch_shapes=[pltpu.VMEM((bm, d), BF16)],
        compiler_params=_params("parallel", "arbitrary"),
        name=name,
    )(x, g.reshape(1, d), w1, w3, w2)


def _row_copy(src_hbm, row, dst, slot, sem):
    return pltpu.make_async_copy(src_hbm.at[pl.ds(row, 1), :], dst.at[pl.ds(slot, 1), :], sem)


ROW_DMA_PRIORITY = 1


def _ffn_expert_kernel(te_ref, tb_ref, tf_ref, rt_ref, x_hbm, g_ref, w1_ref, w3_ref, w2_ref, o_ref,
                       xbuf, sem, xn_ref, *, bm, n_issue):
    i = pl.program_id(0)
    f = pl.program_id(1)
    n_tiles = pl.num_programs(0)
    chunk = bm // n_issue

    def tile_rows(t, r0, n, fn):
        def body(r, carry):
            fn(_row_copy(x_hbm, rt_ref[t * bm + r0 + r], xbuf.at[t % 2], r0 + r, sem.at[t % 2]))
            return carry

        lax.fori_loop(0, n, body, 0)

    def occupied(t):
        return tb_ref[jnp.minimum(t, n_tiles - 1)] == t

    @pl.when(jnp.logical_and(i == 0, f == 0))
    def _():
        tile_rows(i, 0, bm, lambda cp: cp.start(priority=ROW_DMA_PRIORITY))

    @pl.when(jnp.logical_and(jnp.logical_not(occupied(i)), f == 0))
    def _():
        o_ref[...] = jnp.zeros(o_ref.shape, F32)

    @pl.when(occupied(i))
    def _():
        @pl.when(f == 0)
        def _():
            tile_rows(i, 0, bm, lambda cp: cp.wait())
            xn_ref[...] = _rms(xbuf[i % 2], g_ref[...]).astype(BF16)

        @pl.when(jnp.logical_and(f < n_issue, occupied(i + 1)))
        def _():
            tile_rows(i + 1, f * chunk, chunk, lambda cp: cp.start(priority=ROW_DMA_PRIORITY))

        y = _swiglu_tile(xn_ref, w1_ref, w3_ref, w2_ref)

        @pl.when(f == 0)
        def _():
            o_ref[...] = y

        @pl.when(f > 0)
        def _():
            o_ref[...] += y


def _ffn_experts(x, g, row_tok, w1_e, w3_e, w2_e, tile_expert, tile_block, bm, name):
    d = x.shape[1]
    n_rows = row_tok.shape[0]
    ff = w1_e.shape[2]
    bf = _tile(ff, 512, LANES)
    n_f = ff // bf
    n_tiles = n_rows // bm
    tile_f = jnp.where(tile_block == jnp.arange(n_tiles, dtype=jnp.int32), 0, 1).astype(jnp.int32)

    def f_idx(i, f, tf):
        return jnp.where(tf[i] == 0, f, n_f - 1)

    grid_spec = pltpu.PrefetchScalarGridSpec(
        num_scalar_prefetch=4,
        grid=(n_tiles, n_f),
        in_specs=[pl.BlockSpec(memory_space=pl.ANY),
                  pl.BlockSpec((1, d), lambda i, f, te, tb, tf, rt: (0, 0)),
                  pl.BlockSpec((None, d, bf), lambda i, f, te, tb, tf, rt: (te[i], 0, f_idx(i, f, tf))),
                  pl.BlockSpec((None, d, bf), lambda i, f, te, tb, tf, rt: (te[i], 0, f_idx(i, f, tf))),
                  pl.BlockSpec((None, bf, d), lambda i, f, te, tb, tf, rt: (te[i], f_idx(i, f, tf), 0))],
        out_specs=pl.BlockSpec((bm, d), lambda i, f, te, tb, tf, rt: (i, 0)),
        scratch_shapes=[pltpu.VMEM((2, bm, d), F32), pltpu.SemaphoreType.DMA((2,)), pltpu.VMEM((bm, d), BF16)],
    )
    n_issue = max(k for k in range(1, min(n_f, 8) + 1) if bm % k == 0)
    return pl.pallas_call(
        functools.partial(_ffn_expert_kernel, bm=bm, n_issue=n_issue),
        grid_spec=grid_spec,
        out_shape=jax.ShapeDtypeStruct((n_rows, d), F32),
        compiler_params=_params("arbitrary", "arbitrary"),
        name=name,
    )(tile_expert, tile_block, tile_f, row_tok, x, g.reshape(1, d), w1_e, w3_e, w2_e)


def _ln_silu(h, g, b):
    mu = jnp.mean(h, axis=-1, keepdims=True)
    hc = h - mu
    y = hc * lax.rsqrt(jnp.mean(hc * hc, axis=-1, keepdims=True) + EPS) * g + b
    return y * jax.nn.sigmoid(y)


def _conv_prompt_kernel(a_ref, gate_ref, w_ref, cb_ref, lg_ref, lb_ref, o_ref, st_ref, uext_ref, h_ref,
                        *, bt, width, halo, rc):
    t = pl.program_id(1)
    c = a_ref.shape[1]

    @pl.when(t == 0)
    def _():
        uext_ref[0:halo, :] = jnp.zeros((halo, c), F32)

    @pl.when(t > 0)
    def _():
        uext_ref[0:halo, :] = uext_ref[bt:bt + halo, :]

    uext_ref[halo:halo + bt, :] = a_ref[...] * jax.nn.sigmoid(gate_ref[...])

    off = halo - (width - 1)
    for r0 in range(0, bt, rc):
        for c0 in range(0, c, LANES):
            acc = jnp.zeros((rc, LANES), F32)
            for k in range(width):
                acc = acc + w_ref[k:k + 1, c0:c0 + LANES] * uext_ref[r0 + off + k:r0 + off + k + rc, c0:c0 + LANES]
            h_ref[r0:r0 + rc, c0:c0 + LANES] = acc
    o_ref[...] = _ln_silu(h_ref[...] + cb_ref[...], lg_ref[...], lb_ref[...]).astype(o_ref.dtype)

    @pl.when(t == pl.num_programs(1) - 1)
    def _():
        st_ref[...] = uext_ref[bt:bt + halo, :]


def _conv_prompt(h0, n_b, n_t, c, conv_w, conv_b, ln_g, ln_b):
    width = conv_w.shape[0]
    halo = -(-(width - 1) // 8) * 8
    bt = _tile(n_t, 256, 8)
    nt = n_t // bt
    out, st = pl.pallas_call(
        functools.partial(_conv_prompt_kernel, bt=bt, width=width, halo=halo, rc=_tile(bt, 64, 8)),
        grid=(n_b, nt),
        in_specs=[pl.BlockSpec((bt, c), lambda b, t: (b * nt + t, 0)),
                  pl.BlockSpec((bt, c), lambda b, t: (b * nt + t, 1)),
                  pl.BlockSpec((width, c), lambda b, t: (0, 0)),
                  pl.BlockSpec((1, c), lambda b, t: (0, 0)),
                  pl.BlockSpec((1, c), lambda b, t: (0, 0)),
                  pl.BlockSpec((1, c), lambda b, t: (0, 0))],
        out_specs=[pl.BlockSpec((bt, c), lambda b, t: (b * nt + t, 0)),
                   pl.BlockSpec((None, halo, c), lambda b, t: (b, 0, 0))],
        out_shape=[jax.ShapeDtypeStruct((n_b * n_t, c), BF16),
                   jax.ShapeDtypeStruct((n_b, halo, c), F32)],
        scratch_shapes=[pltpu.VMEM((bt + halo, c), F32), pltpu.VMEM((bt, c), F32)],
        compiler_params=_params("parallel", "arbitrary"),
        name="conv_prompt",
    )(h0, h0, conv_w, conv_b.reshape(1, c), ln_g.reshape(1, c), ln_b.reshape(1, c))
    return out, st


def _conv_sample_kernel(a_ref, gate_ref, st_ref, wst_ref, wnew_ref, cb_ref, lg_ref, lb_ref, o_ref, u_ref, h_ref,
                        *, bb, n_t):
    u = a_ref[...] * jax.nn.sigmoid(gate_ref[...])
    u_ref[...] = u
    for b in range(bb):
        past = st_ref[b]
        new = u[b * n_t:(b + 1) * n_t, :]
        for t in range(n_t):
            h_ref[b * n_t + t:b * n_t + t + 1, :] = (
                jnp.sum(past * wst_ref[t], axis=0, keepdims=True)
                + jnp.sum(new * wnew_ref[t], axis=0, keepdims=True))
    o_ref[...] = _ln_silu(h_ref[...] + cb_ref[...], lg_ref[...], lb_ref[...]).astype(o_ref.dtype)


def _conv_sample(h0, row0, n_b, n_t, c, state, conv_w, conv_b, ln_g, ln_b):
    width = conv_w.shape[0]
    n_st = width - 1
    j = jnp.arange(n_st + n_t)[None, :] - jnp.arange(n_t)[:, None]
    w_shift = jnp.where(((j >= 0) & (j < width))[..., None], conv_w[jnp.clip(j, 0, width - 1)], 0.0)
    bb = _tile(n_b, 8, 1)
    rows = bb * n_t
    blk0 = row0 // rows
    assert row0 % rows == 0 and rows % 8 == 0
    out, u = pl.pallas_call(
        functools.partial(_conv_sample_kernel, bb=bb, n_t=n_t),
        grid=(n_b // bb,),
        in_specs=[pl.BlockSpec((rows, c), lambda i: (blk0 + i, 0)),
                  pl.BlockSpec((rows, c), lambda i: (blk0 + i, 1)),
                  pl.BlockSpec((bb, n_st, c), lambda i: (i, 0, 0)),
                  pl.BlockSpec((n_t, n_st, c), lambda i: (0, 0, 0)),
                  pl.BlockSpec((n_t, n_t, c), lambda i: (0, 0, 0)),
                  pl.BlockSpec((1, c), lambda i: (0, 0)),
                  pl.BlockSpec((1, c), lambda i: (0, 0)),
                  pl.BlockSpec((1, c), lambda i: (0, 0))],
        out_specs=[pl.BlockSpec((rows, c), lambda i: (i, 0)),
                   pl.BlockSpec((rows, c), lambda i: (i, 0))],
        out_shape=[jax.ShapeDtypeStruct((n_b * n_t, c), BF16),
                   jax.ShapeDtypeStruct((n_b * n_t, c), F32)],
        scratch_shapes=[pltpu.VMEM((rows, c), F32)],
        compiler_params=_params("parallel"),
        name="conv_sample",
    )(h0, h0, state, w_shift[:, :n_st], w_shift[:, n_st:], conv_b.reshape(1, c), ln_g.reshape(1, c),
      ln_b.reshape(1, c))
    return out, u


def _rope_lanes(x, cs, sn, half):
    lane = lax.broadcasted_iota(jnp.int32, x.shape, 1)
    swapped = jnp.where(lane % (2 * half) < half, pltpu.roll(x, LANES - half, 1), pltpu.roll(x, half, 1))
    return x * cs + swapped * sn


def _mla_prep_kernel(cq_ref, ckv_ref, kr_ref, gq_ref, gkv_ref, wuq_ref, cs_ref, sn_ref,
                     q_ref, ckv_o_ref, kr_o_ref, *, n_heads, half, scale):
    cs, sn = cs_ref[...], sn_ref[...]
    q = _dot(_rms(cq_ref[...], gq_ref[...]).astype(BF16), wuq_ref[...])
    hw = 2 * LANES
    for h in range(n_heads):
        q_ref[:, h * hw:h * hw + LANES] = (q[:, h * hw:h * hw + LANES] * scale).astype(BF16)
        q_ref[:, h * hw + LANES:(h + 1) * hw] = (
            _rope_lanes(q[:, h * hw + LANES:(h + 1) * hw], cs, sn, half) * scale).astype(BF16)
    ckv_o_ref[...] = _rms(ckv_ref[...], gkv_ref[...])
    kr_o_ref[...] = _rope_lanes(kr_ref[...], cs, sn, half)


def _mla_prep(h0, col_q, q_lora, kv_lora, g_q, g_kv, wuq_pad, cos_t, sin_t, n_heads, half, scale):
    m = h0.shape[0]
    bm = _tile(m, 512, 8)
    assert col_q % q_lora == 0 and (col_q + q_lora) % kv_lora == 0 and (col_q + q_lora + kv_lora) % LANES == 0
    hw = 2 * LANES
    return pl.pallas_call(
        functools.partial(_mla_prep_kernel, n_heads=n_heads, half=half, scale=scale),
        grid=(m // bm,),
        in_specs=[pl.BlockSpec((bm, q_lora), lambda i: (i, col_q // q_lora)),
                  pl.BlockSpec((bm, kv_lora), lambda i: (i, (col_q + q_lora) // kv_lora)),
                  pl.BlockSpec((bm, LANES), lambda i: (i, (col_q + q_lora + kv_lora) // LANES)),
                  pl.BlockSpec((1, q_lora), lambda i: (0, 0)),
                  pl.BlockSpec((1, kv_lora), lambda i: (0, 0)),
                  pl.BlockSpec((q_lora, n_heads * hw), lambda i: (0, 0)),
                  pl.BlockSpec((bm, LANES), lambda i: (i, 0)),
                  pl.BlockSpec((bm, LANES), lambda i: (i, 0))],
        out_specs=[pl.BlockSpec((bm, n_heads * hw), lambda i: (i, 0)),
                   pl.BlockSpec((bm, kv_lora), lambda i: (i, 0)),
                   pl.BlockSpec((bm, LANES), lambda i: (i, 0))],
        out_shape=[jax.ShapeDtypeStruct((m, n_heads * hw), BF16),
                   jax.ShapeDtypeStruct((m, kv_lora), F32),
                   jax.ShapeDtypeStruct((m, LANES), F32)],
        compiler_params=_params("parallel"),
        name="mla_prep",
    )(h0, h0, h0, g_q.reshape(1, -1), g_kv.reshape(1, -1), wuq_pad, cos_t, sin_t)


def _mla_kv_kernel(ckv_ref, kr_ref, wuk_ref, wuv_ref, k_ref, v_ref, *, n_heads):
    c = ckv_ref[...].astype(BF16)
    kn = _dot(c, wuk_ref[...])
    v_ref[...] = _dot(c, wuv_ref[...]).astype(BF16)
    kr = kr_ref[...].astype(BF16)
    hw = 2 * LANES
    for h in range(n_heads):
        k_ref[:, h * hw:h * hw + LANES] = kn[:, h * LANES:(h + 1) * LANES].astype(BF16)
        k_ref[:, h * hw + LANES:(h + 1) * hw] = kr


def _mla_kv(ckv, kr_pad, m_p, wuk_flat, wuv_flat, n_heads):
    kv_lora = ckv.shape[1]
    bm = _tile(m_p, 512, 8)
    hw = 2 * LANES
    return pl.pallas_call(
        functools.partial(_mla_kv_kernel, n_heads=n_heads),
        grid=(m_p // bm,),
        in_specs=[pl.BlockSpec((bm, kv_lora), lambda i: (i, 0)),
                  pl.BlockSpec((bm, LANES), lambda i: (i, 0)),
                  pl.BlockSpec(wuk_flat.shape, lambda i: (0, 0)),
                  pl.BlockSpec(wuv_flat.shape, lambda i: (0, 0))],
        out_specs=[pl.BlockSpec((bm, n_heads * hw), lambda i: (i, 0)),
                   pl.BlockSpec((bm, n_heads * LANES), lambda i: (i, 0))],
        out_shape=[jax.ShapeDtypeStruct((m_p, n_heads * hw), BF16),
                   jax.ShapeDtypeStruct((m_p, n_heads * LANES), BF16)],
        compiler_params=_params("parallel"),
        name="mla_kv_expand",
    )(ckv, kr_pad, wuk_flat, wuv_flat)


def _head_matmul_kernel(a_ref, w_ref, o_ref):
    o_ref[...] = _dot(a_ref[...].astype(BF16), w_ref[...]).astype(o_ref.dtype)


def _head_matmul(a, row0, m, col0, col_stride, w, out_dtype, name):
    n_heads, k, n = w.shape
    assert col0 % k == 0 and col_stride % k == 0 and row0 % m == 0
    return pl.pallas_call(
        _head_matmul_kernel,
        grid=(n_heads,),
        in_specs=[pl.BlockSpec((m, k), lambda h: (row0 // m, (col0 + h * col_stride) // k)),
                  pl.BlockSpec((None, k, n), lambda h: (h, 0, 0))],
        out_specs=pl.BlockSpec((m, n), lambda h: (0, h)),
        out_shape=jax.ShapeDtypeStruct((m, n_heads * n), out_dtype),
        compiler_params=_params("parallel"),
        name=name,
    )(a, w)


def _mla_flash_kernel(q_ref, k_ref, v_ref, o_ref, m_ref, l_ref, acc_ref, *, blk, hp):
    qi = pl.program_id(2)
    hw = 2 * LANES
    m_ref[...] = jnp.full(m_ref.shape, -jnp.inf, F32)
    l_ref[...] = jnp.zeros(l_ref.shape, F32)
    acc_ref[...] = jnp.zeros(acc_ref.shape, F32)

    def step(kb, masked):
        r0 = pl.multiple_of(kb * blk, blk)
        for j in range(hp):
            s = _dot_nt(q_ref[:, j * hw:(j + 1) * hw], k_ref[pl.ds(r0, blk), j * hw:(j + 1) * hw])
            if masked:
                row = lax.broadcasted_iota(jnp.int32, s.shape, 0)
                col = lax.broadcasted_iota(jnp.int32, s.shape, 1)
                s = jnp.where(col <= row, s, -jnp.inf)
            m_old = m_ref[j]
            m_new = jnp.maximum(m_old, jnp.max(s, axis=-1, keepdims=True))
            p = jnp.exp(s - m_new)
            corr = jnp.exp(m_old - m_new)
            l_ref[j] = l_ref[j] * corr + jnp.sum(p, axis=-1, keepdims=True)
            acc_ref[j] = acc_ref[j] * corr + _dot(p.astype(BF16), v_ref[pl.ds(r0, blk), j * LANES:(j + 1) * LANES])
            m_ref[j] = m_new

    step(qi, True)

    def body(kb, carry):
        step(kb, False)
        return carry

    lax.fori_loop(0, qi, body, 0)
    for j in range(hp):
        o_ref[:, j * LANES:(j + 1) * LANES] = (acc_ref[j] / l_ref[j]).astype(o_ref.dtype)


def _mla_flash(q_pad, k_pad, v, n_b, n_t, n_heads):
    blk = _tile(n_t, 512, 8)
    nq = n_t // blk
    hp = _tile(n_heads, 2, 1)
    hw = 2 * LANES
    return pl.pallas_call(
        functools.partial(_mla_flash_kernel, blk=blk, hp=hp),
        grid=(n_b, n_heads // hp, nq),
        in_specs=[pl.BlockSpec((blk, hp * hw), lambda b, h, i: (b * nq + i, h)),
                  pl.BlockSpec((n_t, hp * hw), lambda b, h, i: (b, h)),
                  pl.BlockSpec((n_t, hp * LANES), lambda b, h, i: (b, h))],
        out_specs=pl.BlockSpec((blk, hp * LANES), lambda b, h, i: (b * nq + i, h)),
        out_shape=jax.ShapeDtypeStruct((n_b * n_t, n_heads * LANES), BF16),
        scratch_shapes=[pltpu.VMEM((hp, blk, 1), F32), pltpu.VMEM((hp, blk, 1), F32),
                        pltpu.VMEM((hp, blk, LANES), F32)],
        compiler_params=_params("parallel", "parallel", "arbitrary"),
        name="mla_prompt_attn",
    )(q_pad, k_pad, v)


def _first_slot(i):
    return i % 2


def _loop_slot(s):
    return 2 + (s - 1) % 2


def _mla_decode_kernel(pt_ref, ql_ref, qr_ref, cnew_ref, rnew_ref, ckv_hbm, kr_hbm, o_ref,
                       cbuf, rbuf, csem, rsem, m_ref, l_ref, acc_ref, *, n_seq, n_pg, n_steps, n_t, n_heads):
    i = pl.program_id(0)

    def copies(ii, step, slot):
        out = []
        for q in range(n_seq):
            for g in range(n_pg):
                pg = pt_ref[ii * n_seq + q, step * n_pg + g]
                out.append(pltpu.make_async_copy(ckv_hbm.at[pg], cbuf.at[slot, q * n_pg + g], csem.at[slot]))
                out.append(pltpu.make_async_copy(kr_hbm.at[pg], rbuf.at[slot, q * n_pg + g], rsem.at[slot]))
        return out

    def start(ii, step, slot):
        for cp in copies(ii, step, slot):
            cp.start()

    def wait(ii, step, slot):
        for cp in copies(ii, step, slot):
            cp.wait()

    @pl.when(i == 0)
    def _():
        start(i, 0, _first_slot(i))

    @pl.when(i + 1 < pl.num_programs(0))
    def _():
        start(i + 1, 0, _first_slot(i + 1))

    if n_steps > 1:
        start(i, 1, _loop_slot(1))

    rows = ql_ref.shape[1]
    t_row = lax.broadcasted_iota(jnp.int32, (rows, 1), 0) // n_heads
    for q in range(n_seq):
        qlf, qrf = ql_ref[q].astype(F32), qr_ref[q].astype(F32)
        m = jnp.full((rows, 1), -jnp.inf, F32)
        l = jnp.zeros((rows, 1), F32)
        acc = jnp.zeros(acc_ref.shape[1:], F32)
        for t in range(n_t):
            c = cnew_ref[q, t:t + 1, :]
            s = (jnp.sum(qlf * c, axis=-1, keepdims=True)
                 + jnp.sum(qrf * rnew_ref[q, t:t + 1, :], axis=-1, keepdims=True))
            s = jnp.where(t <= t_row, s, -jnp.inf)
            m_new = jnp.maximum(m, s)
            corr = jnp.exp(m - m_new)
            p = jnp.exp(s - m_new)
            l = l * corr + p
            acc = acc * corr + p * c
            m = m_new
        m_ref[q] = m
        l_ref[q] = l
        acc_ref[q] = acc

    def compute(slot):
        rope = rbuf.shape[2]
        for q in range(n_seq):
            ql = ql_ref[q]
            qr = qr_ref[q][:, :rope]
            ckv = [cbuf[slot, q * n_pg + g].astype(BF16) for g in range(n_pg)]
            s = jnp.concatenate(
                [_dot_nt(ql, c) + _dot(qr, rbuf[slot, q * n_pg + g].astype(BF16)) for g, c in enumerate(ckv)],
                axis=-1)
            m_old = m_ref[q]
            m_new = jnp.maximum(m_old, jnp.max(s, axis=-1, keepdims=True))
            p = jnp.exp(s - m_new)
            corr = jnp.exp(m_old - m_new)
            l_ref[q] = l_ref[q] * corr + jnp.sum(p, axis=-1, keepdims=True)
            acc = acc_ref[q] * corr
            pg = ckv[0].shape[0]
            for g, c in enumerate(ckv):
                acc = acc + _dot(p[:, g * pg:(g + 1) * pg].astype(BF16), c)
            acc_ref[q] = acc
            m_ref[q] = m_new

    wait(i, 0, _first_slot(i))
    compute(_first_slot(i))

    def body(s, carry):
        @pl.when(s + 1 < n_steps)
        def _():
            start(i, s + 1, _loop_slot(s + 1))

        wait(i, s, _loop_slot(s))
        compute(_loop_slot(s))
        return carry

    lax.fori_loop(1, n_steps, body, 0)
    for q in range(n_seq):
        o_ref[q] = acc_ref[q] / l_ref[q]


def _mla_decode(page_table, q_lat, q_rope, ckv_new, kr_new, cache_ckv, cache_kr_t, n_heads):
    n_b, rows, c = q_lat.shape
    n_t = ckv_new.shape[1]
    n_pages = page_table.shape[1]
    page = cache_ckv.shape[1]
    rope = cache_kr_t.shape[1]
    n_pg = _tile(n_pages, 8, 1)
    n_seq = _tile(n_b, 2, 1)
    grid_spec = pltpu.PrefetchScalarGridSpec(
        num_scalar_prefetch=1,
        grid=(n_b // n_seq,),
        in_specs=[pl.BlockSpec((n_seq, rows, c), lambda i, pt: (i, 0, 0)),
                  pl.BlockSpec((n_seq, rows, LANES), lambda i, pt: (i, 0, 0)),
                  pl.BlockSpec((n_seq, n_t, c), lambda i, pt: (i, 0, 0)),
                  pl.BlockSpec((n_seq, n_t, LANES), lambda i, pt: (i, 0, 0)),
                  pl.BlockSpec(memory_space=pl.ANY),
                  pl.BlockSpec(memory_space=pl.ANY)],
        out_specs=pl.BlockSpec((n_seq, rows, c), lambda i, pt: (i, 0, 0)),
        scratch_shapes=[pltpu.VMEM((4, n_seq * n_pg, page, c), F32), pltpu.VMEM((4, n_seq * n_pg, rope, page), F32),
                        pltpu.SemaphoreType.DMA((4,)), pltpu.SemaphoreType.DMA((4,)),
                        pltpu.VMEM((n_seq, rows, 1), F32), pltpu.VMEM((n_seq, rows, 1), F32),
                        pltpu.VMEM((n_seq, rows, c), F32)],
    )
    return pl.pallas_call(
        functools.partial(_mla_decode_kernel, n_seq=n_seq, n_pg=n_pg, n_steps=n_pages // n_pg, n_t=n_t,
                          n_heads=n_heads),
        grid_spec=grid_spec,
        out_shape=jax.ShapeDtypeStruct((n_b, rows, c), F32),
        compiler_params=_params("arbitrary"),
        name="mla_sample_attn",
    )(page_table, q_lat, q_rope, ckv_new, kr_new, cache_ckv, cache_kr_t)


SB_DEAD_TAIL = -104.0


def _sb_block(q, k, v, u_ref, tail, mask):
    z = _dot_nt(q, k)
    log_beta = jnp.minimum(z, 0.0) - jnp.log(1.0 + jnp.exp(-jnp.abs(z)))
    log_1m = log_beta - z
    if mask is not None:
        log_1m = jnp.where(mask, log_1m, 0.0)
    hi = lax.bitcast_convert_type(lax.bitcast_convert_type(log_1m, jnp.uint32) & jnp.uint32(0xFFFF0000), F32)
    later = _dot(hi.astype(BF16), u_ref[...]) + _dot((log_1m - hi).astype(BF16), u_ref[...]) + tail
    w = jnp.exp(log_beta + later)
    if mask is not None:
        w = jnp.where(mask, w, 0.0)
    return _dot(w.astype(BF16), v), tail + jnp.sum(log_1m, axis=-1, keepdims=True)


def _suffix_matrix(n):
    j = jnp.arange(n)
    return (j[:, None] > j[None, :]).astype(BF16)


def _alive(tail_ref):
    return (jnp.max(tail_ref[...]) > SB_DEAD_TAIL).astype(jnp.int32)


def _sb_prompt_kernel(q_ref, k_ref, v_ref, u_ref, o_ref, acc_ref, tail_ref, *, blk, hp, dh, scale):
    qi = pl.program_id(3)

    def q_head(j):
        return (q_ref[:, j * dh:(j + 1) * dh] * scale).astype(BF16)

    def kv(kb):
        r0 = pl.multiple_of(kb * blk, blk)
        return k_ref[pl.ds(r0, blk), :].astype(BF16), v_ref[pl.ds(r0, blk), :].astype(BF16)

    row = lax.broadcasted_iota(jnp.int32, (blk, blk), 0)
    col = lax.broadcasted_iota(jnp.int32, (blk, blk), 1)
    k, v = kv(qi)
    for j in range(hp):
        acc, tail = _sb_block(q_head(j), k, v, u_ref, jnp.zeros((blk, 1), F32), col < row)
        acc_ref[j] = acc
        tail_ref[j] = tail

    def cond(c):
        return jnp.logical_and(c[0] < qi, c[1] == 1)

    def body(c):
        k, v = kv(qi - 1 - c[0])
        for j in range(hp):
            acc, tail = _sb_block(q_head(j), k, v, u_ref, tail_ref[j], None)
            acc_ref[j] += acc
            tail_ref[j] = tail
        return c[0] + 1, _alive(tail_ref)

    lax.while_loop(cond, body, (jnp.int32(0), _alive(tail_ref)))
    for j in range(hp):
        o_ref[:, j * dh:(j + 1) * dh] = acc_ref[j].astype(o_ref.dtype)


def _sb_prompt(h1, n_b, n_t, n_heads, n_kv, dh, scale):
    blk = _tile(n_t, 256, 8)
    nq = n_t // blk
    group = n_heads // n_kv
    hp = _tile(group, 4, 1)
    gp = group // hp
    return pl.pallas_call(
        functools.partial(_sb_prompt_kernel, blk=blk, hp=hp, dh=dh, scale=scale),
        grid=(n_b, n_kv, gp, nq),
        in_specs=[pl.BlockSpec((blk, hp * dh), lambda b, g, r, i: (b * nq + i, g * gp + r)),
                  pl.BlockSpec((n_t, dh), lambda b, g, r, i: (b, n_heads + g)),
                  pl.BlockSpec((n_t, dh), lambda b, g, r, i: (b, n_heads + n_kv + g)),
                  pl.BlockSpec((blk, blk), lambda b, g, r, i: (0, 0))],
        out_specs=pl.BlockSpec((blk, hp * dh), lambda b, g, r, i: (b * nq + i, g * gp + r)),
        out_shape=jax.ShapeDtypeStruct((n_b * n_t, n_heads * dh), BF16),
        scratch_shapes=[pltpu.VMEM((hp, blk, dh), F32), pltpu.VMEM((hp, blk, 1), F32)],
        compiler_params=_params("parallel", "parallel", "parallel", "arbitrary"),
        name="sb_prompt_attn",
    )(h1, h1, h1, _suffix_matrix(blk))


def _sb_decode_kernel(pt_ref, q_ref, kn_ref, vn_ref, u_ref, k_hbm, v_hbm, o_ref,
                      kbuf, vbuf, ksem, vsem, acc_ref, tail_ref, alive_ref,
                      *, n_pg, n_steps, n_pages, page, n_t, n_kv, group, dh, scale):
    b = pl.program_id(0)

    def copies(bb, step, slot):
        out = []
        for g in range(n_pg):
            pg = pt_ref[bb, n_pages - 1 - (step * n_pg + g)]
            out.append(pltpu.make_async_copy(k_hbm.at[pg], kbuf.at[slot, g], ksem.at[slot]))
            out.append(pltpu.make_async_copy(v_hbm.at[pg], vbuf.at[slot, g], vsem.at[slot]))
        return out

    def start(bb, step, slot):
        for cp in copies(bb, step, slot):
            cp.start()

    def wait(bb, step, slot):
        for cp in copies(bb, step, slot):
            cp.wait()

    @pl.when(b == 0)
    def _():
        start(b, 0, _first_slot(b))

    @pl.when(b + 1 < pl.num_programs(0))
    def _():
        start(b + 1, 0, _first_slot(b + 1))

    if n_steps > 1:
        start(b, 1, _loop_slot(1))

    t_row = lax.broadcasted_iota(jnp.int32, (n_t * group, 1), 0) // group
    for g in range(n_kv):
        q = (q_ref[g] * scale).astype(BF16).astype(F32)
        tail = jnp.zeros((n_t * group, 1), F32)
        acc = jnp.zeros((n_t * group, dh), F32)
        for t in reversed(range(n_t)):
            z = jnp.sum(q * kn_ref[t:t + 1, g * dh:(g + 1) * dh], axis=-1, keepdims=True)
            sp = jnp.log1p(jnp.exp(-jnp.abs(z)))
            mask = t < t_row
            w = jnp.where(mask, jnp.exp(jnp.minimum(z, 0.0) - sp + tail), 0.0)
            acc = acc + w * vn_ref[t:t + 1, g * dh:(g + 1) * dh]
            tail = tail + jnp.where(mask, jnp.minimum(-z, 0.0) - sp, 0.0)
        acc_ref[g] = acc
        tail_ref[g] = tail
    alive_ref[0] = 1

    def compute(slot):
        for g in range(n_pg):
            @pl.when(alive_ref[0] == 1)
            def _():
                for grp in range(n_kv):
                    q = (q_ref[grp] * scale).astype(BF16)
                    d, tail = _sb_block(q, kbuf[slot, g, pl.ds(grp, page, stride=n_kv), :].astype(BF16),
                                        vbuf[slot, g, pl.ds(grp, page, stride=n_kv), :].astype(BF16),
                                        u_ref, tail_ref[grp], None)
                    acc_ref[grp] += d
                    tail_ref[grp] = tail
                alive_ref[0] = _alive(tail_ref)

    wait(b, 0, _first_slot(b))
    compute(_first_slot(b))

    if n_steps > 1:
        def cond(c):
            return jnp.logical_and(c[0] < n_steps, c[1] == 1)

        def body(c):
            s = c[0]

            @pl.when(s + 1 < n_steps)
            def _():
                start(b, s + 1, _loop_slot(s + 1))

            wait(b, s, _loop_slot(s))
            compute(_loop_slot(s))
            return s + 1, alive_ref[0]

        s_end, _ = lax.while_loop(cond, body, (jnp.int32(1), alive_ref[0]))

        @pl.when(s_end < n_steps)
        def _():
            wait(b, s_end, _loop_slot(s_end))

    o_ref[...] = acc_ref[...]


def _sb_decode(page_table, q, k_new, v_new, cache_k, cache_v, scale):
    n_b, n_kv, rows, dh = q.shape
    n_t = k_new.shape[1]
    group = rows // n_t
    n_pages = page_table.shape[1]
    page = cache_k.shape[1] // n_kv
    n_pg = _tile(n_pages, 4, 1)
    grid_spec = pltpu.PrefetchScalarGridSpec(
        num_scalar_prefetch=1,
        grid=(n_b,),
        in_specs=[pl.BlockSpec((None, n_kv, rows, dh), lambda b, pt: (b, 0, 0, 0)),
                  pl.BlockSpec((None, n_t, n_kv * dh), lambda b, pt: (b, 0, 0)),
                  pl.BlockSpec((None, n_t, n_kv * dh), lambda b, pt: (b, 0, 0)),
                  pl.BlockSpec((page, page), lambda b, pt: (0, 0)),
                  pl.BlockSpec(memory_space=pl.ANY),
                  pl.BlockSpec(memory_space=pl.ANY)],
        out_specs=pl.BlockSpec((None, n_kv, rows, dh), lambda b, pt: (b, 0, 0, 0)),
        scratch_shapes=[pltpu.VMEM((4, n_pg, page * n_kv, dh), F32), pltpu.VMEM((4, n_pg, page * n_kv, dh), F32),
                        pltpu.SemaphoreType.DMA((4,)), pltpu.SemaphoreType.DMA((4,)),
                        pltpu.VMEM((n_kv, rows, dh), F32), pltpu.VMEM((n_kv, rows, 1), F32),
                        pltpu.SMEM((1,), jnp.int32)],
    )
    return pl.pallas_call(
        functools.partial(_sb_decode_kernel, n_pg=n_pg, n_steps=n_pages // n_pg, n_pages=n_pages, page=page,
                          n_t=n_t, n_kv=n_kv, group=group, dh=dh, scale=scale),
        grid_spec=grid_spec,
        out_shape=jax.ShapeDtypeStruct((n_b, n_kv, rows, dh), F32),
        compiler_params=_params("arbitrary"),
        name="sb_sample_attn",
    )(page_table, q, k_new, v_new, _suffix_matrix(page), cache_k, cache_v)


def _router_kernel(x_ref, g_ref, wr_ref, gate_ref, sel_ref, *, n_exp):
    xn = _rms(x_ref[...], g_ref[...])
    logits = jnp.concatenate(
        [jnp.sum(xn * wr_ref[e:e + 1, :], axis=-1, keepdims=True) for e in range(n_exp)], axis=-1)
    e_id = lax.broadcasted_iota(jnp.int32, logits.shape, 1)
    m1 = jnp.max(logits, axis=-1, keepdims=True)
    i1 = jnp.min(jnp.where(logits == m1, e_id, n_exp), axis=-1, keepdims=True)
    rest = jnp.where(e_id == i1, -jnp.inf, logits)
    m2 = jnp.max(rest, axis=-1, keepdims=True)
    i2 = jnp.min(jnp.where(rest == m2, e_id, n_exp), axis=-1, keepdims=True)
    e2 = jnp.exp(m2 - m1)
    den = 1.0 + e2
    gate_ref[...] = jnp.where(e_id == i1, 1.0 / den, 0.0) + jnp.where(e_id == i2, e2 / den, 0.0)
    sel_ref[...] = jnp.where((e_id == i1) | (e_id == i2), 1.0, 0.0)


def _router(x, g, w_router):
    m, d = x.shape
    n_exp = w_router.shape[1]
    bm = _tile(m, 512, 8)
    return pl.pallas_call(
        functools.partial(_router_kernel, n_exp=n_exp),
        grid=(m // bm,),
        in_specs=[pl.BlockSpec((bm, d), lambda i: (i, 0)),
                  pl.BlockSpec((1, d), lambda i: (0, 0)),
                  pl.BlockSpec((n_exp, d), lambda i: (0, 0))],
        out_specs=[pl.BlockSpec((bm, n_exp), lambda i: (i, 0))] * 2,
        out_shape=[jax.ShapeDtypeStruct((m, n_exp), F32)] * 2,
        compiler_params=_params("parallel"),
        name="moe_router",
    )(x, g.reshape(1, d), w_router.T)


def _row_index_kernel(p0_ref, p1_ref, o_ref):
    def zero(r, carry):
        o_ref[r] = 0
        return carry

    def put(t, carry):
        o_ref[p0_ref[t]] = t
        o_ref[p1_ref[t]] = t
        return carry

    lax.fori_loop(0, o_ref.shape[0], zero, 0, unroll=16)
    lax.fori_loop(0, p0_ref.shape[0], put, 0, unroll=8)


def _row_index(pos0, pos1, n_rows):
    return pl.pallas_call(
        _row_index_kernel,
        in_specs=[pl.BlockSpec(memory_space=pltpu.SMEM)] * 2,
        out_specs=pl.BlockSpec(memory_space=pltpu.SMEM),
        out_shape=jax.ShapeDtypeStruct((n_rows,), jnp.int32),
        name="moe_row_index",
    )(pos0, pos1)


def _combine_kernel(p0_ref, p1_ref, x_ref, g_ref, g0_ref, g1_ref, y_hbm, o_ref, buf_ref, sem, *, bm):
    i = pl.program_id(0)

    def tile_rows(t, fn):
        def body(r, carry):
            fn(_row_copy(y_hbm, p0_ref[t * bm + r], buf_ref.at[t % 2, 0], r, sem.at[t % 2]), 0)
            fn(_row_copy(y_hbm, p1_ref[t * bm + r], buf_ref.at[t % 2, 1], r, sem.at[t % 2]), 1)
            return carry

        lax.fori_loop(0, bm, body, 0)

    @pl.when(i == 0)
    def _():
        tile_rows(i, lambda cp, thread: cp.start(priority=thread))

    @pl.when(i + 1 < pl.num_programs(0))
    def _():
        tile_rows(i + 1, lambda cp, thread: cp.start(priority=thread))

    tile_rows(i, lambda cp, thread: cp.wait())
    y = g0_ref[...] * buf_ref[i % 2, 0] + g1_ref[...] * buf_ref[i % 2, 1]
    o_ref[...] = _rms(x_ref[...] + y, g_ref[...])


def _combine_norm(x, g, ys, pos0, pos1, g0, g1, bm):
    m, d = x.shape
    grid_spec = pltpu.PrefetchScalarGridSpec(
        num_scalar_prefetch=2,
        grid=(m // bm,),
        in_specs=[pl.BlockSpec((bm, d), lambda i, p0, p1: (i, 0)),
                  pl.BlockSpec((1, d), lambda i, p0, p1: (0, 0)),
                  pl.BlockSpec((bm, 1), lambda i, p0, p1: (i, 0)),
                  pl.BlockSpec((bm, 1), lambda i, p0, p1: (i, 0)),
                  pl.BlockSpec(memory_space=pl.ANY)],
        out_specs=pl.BlockSpec((bm, d), lambda i, p0, p1: (i, 0)),
        scratch_shapes=[pltpu.VMEM((2, 2, bm, d), F32), pltpu.SemaphoreType.DMA((2,))],
    )
    return pl.pallas_call(
        functools.partial(_combine_kernel, bm=bm),
        grid_spec=grid_spec,
        out_shape=jax.ShapeDtypeStruct((m, d), F32),
        compiler_params=_params("arbitrary"),
        name="moe_combine_norm",
    )(pos0, pos1, x, g.reshape(1, d), g0, g1, ys)


def _moe_final(x, g_ffn, g_final, w_router, w1_e, w3_e, w2_e):
    m, d = x.shape
    n_exp = w_router.shape[1]
    bm = _tile(m, 512, 8)
    gate, sel = _router(x, g_ffn, w_router)
    sel = sel > 0.0
    cnt = jnp.cumsum(sel.astype(jnp.int32), axis=0)
    n_e = cnt[-1]
    padded = (n_e + bm - 1) // bm * bm
    ends = jnp.cumsum(padded)
    dest = (ends - padded)[None, :] + cnt - 1
    n_rows = (TOP_K * m + n_exp * (bm - 1)) // bm * bm
    n_tiles = n_rows // bm
    n_used = jnp.maximum(ends[-1] // bm, 1)
    tile_block = jnp.minimum(jnp.arange(n_tiles, dtype=jnp.int32), n_used - 1)
    tile_expert = jnp.minimum(jnp.sum(tile_block[:, None] * bm >= ends[None, :], axis=1), n_exp - 1)
    first = jnp.argmax(sel, axis=1)[:, None]
    last = (n_exp - 1 - jnp.argmax(sel[:, ::-1], axis=1))[:, None]
    pos0 = jnp.take_along_axis(dest, first, axis=1)[:, 0].astype(jnp.int32)
    pos1 = jnp.take_along_axis(dest, last, axis=1)[:, 0].astype(jnp.int32)
    g0 = jnp.take_along_axis(gate, first, axis=1)
    g1 = jnp.take_along_axis(gate, last, axis=1)

    row_tok = _row_index(pos0, pos1, n_rows)
    ys = _ffn_experts(x, g_ffn, row_tok, w1_e, w3_e, w2_e, tile_expert.astype(jnp.int32), tile_block, bm,
                      "moe_experts")
    return _combine_norm(x, g_final, ys, pos0, pos1, g0, g1, _tile(m, 256, 8))


def kernel(x_prompt, x_sample, state_conv0, cache_mla_ckv, cache_mla_krope, cache_sb_k, cache_sb_v, page_table,
           ln_mix0, w_in0, conv_w, conv_b, conv_ln_g, conv_ln_b, q_norm_g, kv_norm_g, w_uq, w_uk, w_uv, w_out0,
           ln_ffn0, w1_d, w3_d, w2_d, ln_mix1, w_in1, w_out1, ln_ffn1, w_router, w1_e, w3_e, w2_e, ln_final):
    n_bp, n_tp, d = x_prompt.shape
    n_bs, n_ts, _ = x_sample.shape
    m_p, m_s = n_bp * n_tp, n_bs * n_ts
    c_conv = conv_w.shape[1]
    q_lora, n_hm, qk_dim = w_uq.shape
    kv_lora, _, nope = w_uk.shape
    v_dim = w_uv.shape[2]
    rope = qk_dim - nope
    half = rope // 2
    page = cache_mla_ckv.shape[1]
    past_len = page_table.shape[1] * page
    n_kv, dh = cache_sb_k.shape[2], cache_sb_k.shape[3]
    n_hs = w_out1.shape[0] // dh
    group = n_hs // n_kv
    assert nope == LANES and v_dim == LANES and dh == LANES and rope <= LANES

    x = jnp.concatenate([x_prompt.reshape(m_p, d), x_sample.reshape(m_s, d)], axis=0)

    d_in0 = w_in0.shape[1]
    d_in0_pad = -(-d_in0 // 640) * 640 if d_in0 > 640 else -(-d_in0 // LANES) * LANES
    w_in0_p = jnp.pad(w_in0, ((0, 0), (0, d_in0_pad - d_in0))).astype(BF16)
    hw = 2 * LANES
    wuq_pad = jnp.pad(w_uq, ((0, 0), (0, 0), (0, hw - qk_dim))).reshape(q_lora, n_hm * hw).astype(BF16)
    wuk_flat = w_uk.reshape(kv_lora, n_hm * nope).astype(BF16)
    wuv_flat = w_uv.reshape(kv_lora, n_hm * v_dim).astype(BF16)
    wuk_t = jnp.transpose(w_uk, (1, 2, 0)).astype(BF16)
    wuv_h = jnp.transpose(w_uv, (1, 0, 2)).astype(BF16)
    w_out0_b = w_out0.astype(BF16)
    w1_db, w3_db, w2_db = w1_d.astype(BF16), w3_d.astype(BF16), w2_d.astype(BF16)
    w_in1_b, w_out1_b = w_in1.astype(BF16), w_out1.astype(BF16)
    w1_eb, w3_eb, w2_eb = w1_e.astype(BF16), w3_e.astype(BF16), w2_e.astype(BF16)

    pos = jnp.concatenate([jnp.tile(jnp.arange(n_tp, dtype=F32), n_bp),
                           jnp.tile(past_len + jnp.arange(n_ts, dtype=F32), n_bs)])
    freqs = ROPE_THETA ** (-jnp.arange(half, dtype=F32) / half)
    ang = pos[:, None] * freqs[None, :]
    zpad = jnp.zeros((m_p + m_s, LANES - rope), F32)
    cos_t = jnp.concatenate([jnp.cos(ang), jnp.cos(ang), zpad], axis=1)
    sin_t = jnp.concatenate([-jnp.sin(ang), jnp.sin(ang), zpad], axis=1)

    h0 = _norm_matmul(x, ln_mix0, w_in0_p, "in_proj0")
    conv_p, st_p = _conv_prompt(h0, n_bp, n_tp, c_conv, conv_w, conv_b, conv_ln_g, conv_ln_b)
    conv_s, u_s = _conv_sample(h0, m_p, n_bs, n_ts, c_conv, state_conv0, conv_w, conv_b, conv_ln_g, conv_ln_b)
    n_state = conv_w.shape[0] - 1
    conv_state_p = st_p[:, st_p.shape[1] - n_state:, :]
    conv_state_s = jnp.concatenate([state_conv0, u_s.reshape(n_bs, n_ts, c_conv)], axis=1)[:, -n_state:, :]

    mla_scale = float(qk_dim) ** -0.5
    q_pad, ckv, kr_pad = _mla_prep(h0, 2 * c_conv, q_lora, kv_lora, q_norm_g, kv_norm_g, wuq_pad, cos_t, sin_t,
                                   n_hm, half, mla_scale)
    k_pad, v_p = _mla_kv(ckv, kr_pad, m_p, wuk_flat, wuv_flat, n_hm)
    o_p = _mla_flash(q_pad, k_pad, v_p, n_bp, n_tp, n_hm)

    q_lat = _head_matmul(q_pad, m_p, m_s, 0, hw, wuk_t, BF16, "mla_q_absorb")
    q_lat = q_lat.reshape(n_bs, n_ts * n_hm, kv_lora)
    q_rope_s = q_pad[m_p:].reshape(n_bs, n_ts * n_hm, hw)[:, :, LANES:]
    o_lat = _mla_decode(page_table, q_lat, q_rope_s, ckv[m_p:].reshape(n_bs, n_ts, kv_lora),
                        kr_pad[m_p:].reshape(n_bs, n_ts, LANES), cache_mla_ckv,
                        jnp.transpose(cache_mla_krope, (0, 2, 1)), n_hm)
    o_s = _head_matmul(o_lat.reshape(m_s, n_hm * kv_lora), 0, m_s, 0, kv_lora, wuv_h, BF16, "mla_v_absorb")

    conv_out = jnp.concatenate([conv_p, conv_s], axis=0)
    mla_out = jnp.concatenate([o_p, o_s], axis=0)
    x = _matmul_res([conv_out, mla_out], [w_out0_b[:c_conv], w_out0_b[c_conv:]], x, "out_proj0")
    x = _ffn_dense(x, ln_ffn0, w1_db, w3_db, w2_db, "ffn0")

    h1 = _norm_matmul(x, ln_mix1, w_in1_b, "in_proj1")
    sb_scale = float(dh) ** -0.5
    so_p = _sb_prompt(h1, n_bp, n_tp, n_hs, n_kv, dh, sb_scale)
    d_q = n_hs * dh
    h1_s = h1[m_p:]
    q_s = jnp.transpose(h1_s[:, :d_q].reshape(n_bs, n_ts, n_kv, group, dh), (0, 2, 1, 3, 4))
    q_s = q_s.reshape(n_bs, n_kv, n_ts * group, dh)
    k_s = h1_s[:, d_q:d_q + n_kv * dh].reshape(n_bs, n_ts, n_kv * dh)
    v_s = h1_s[:, d_q + n_kv * dh:].reshape(n_bs, n_ts, n_kv * dh)
    so_s = _sb_decode(page_table, q_s, k_s, v_s, cache_sb_k.reshape(-1, page * n_kv, dh),
                      cache_sb_v.reshape(-1, page * n_kv, dh), sb_scale)
    so_s = jnp.transpose(so_s.reshape(n_bs, n_kv, n_ts, group, dh), (0, 2, 1, 3, 4)).reshape(m_s, d_q)
    sb_out = jnp.concatenate([so_p, so_s.astype(BF16)], axis=0)
    x = _matmul_res([sb_out], [w_out1_b], x, "out_proj1")

    y = _moe_final(x, ln_ffn1, ln_final, w_router, w1_eb, w3_eb, w2_eb)

    return (y[:m_p].reshape(n_bp, n_tp, d), y[m_p:].reshape(n_bs, n_ts, d),
            conv_state_p, ckv[:m_p].reshape(n_bp, n_tp, kv_lora), kr_pad[:m_p, :rope].reshape(n_bp, n_tp, rope),
            h1[:m_p, d_q:d_q + n_kv * dh].reshape(n_bp, n_tp, n_kv, dh),
            h1[:m_p, d_q + n_kv * dh:].reshape(n_bp, n_tp, n_kv, dh),
            conv_state_s, ckv[m_p:].reshape(n_bs, n_ts, kv_lora), kr_pad[m_p:, :rope].reshape(n_bs, n_ts, rope),
            k_s.reshape(n_bs, n_ts, n_kv, dh), v_s.reshape(n_bs, n_ts, n_kv, dh))
```

```python
import functools

import jax
import jax.numpy as jnp
from jax import lax
from jax.experimental import pallas as pl
from jax.experimental.pallas import tpu as pltpu

F32 = jnp.float32
BF16 = jnp.bfloat16
EPS = 1e-6
ROPE_THETA = 10000.0
TOP_K = 2

LANES = 128
MXU_DIM = 256
VMEM_LIMIT = 56 * 1024 * 1024


def _tile(n, pref, align):
    t = min(pref, n) // align * align
    while t >= align:
        if n % t == 0:
            return t
        t -= align
    return n


def _params(*sem):
    return pltpu.CompilerParams(dimension_semantics=sem, vmem_limit_bytes=VMEM_LIMIT)


def _rms(x, g):
    return x * lax.rsqrt(jnp.mean(x * x, axis=-1, keepdims=True) + EPS) * g


def _dot(a, b):
    return jnp.dot(a, b, preferred_element_type=F32)


def _dot_nt(a, b):
    return lax.dot_general(a, b, (((1,), (1,)), ((), ())), preferred_element_type=F32)


def _norm_matmul_kernel(x_ref, g_ref, w_ref, o_ref, xn_ref):
    @pl.when(pl.program_id(1) == 0)
    def _():
        xn_ref[...] = _rms(x_ref[...], g_ref[...]).astype(BF16)

    o_ref[...] = _dot(xn_ref[...], w_ref[...]).astype(o_ref.dtype)


def _norm_matmul(x, g, w, name):
    m, k = x.shape
    n = w.shape[1]
    bm = _tile(m, 512, 8)
    bn = _tile(n, 640, LANES)
    return pl.pallas_call(
        _norm_matmul_kernel,
        grid=(m // bm, n // bn),
        in_specs=[pl.BlockSpec((bm, k), lambda i, j: (i, 0)),
                  pl.BlockSpec((1, k), lambda i, j: (0, 0)),
                  pl.BlockSpec((k, bn), lambda i, j: (0, j))],
        out_specs=pl.BlockSpec((bm, bn), lambda i, j: (i, j)),
        out_shape=jax.ShapeDtypeStruct((m, n), F32),
        scratch_shapes=[pltpu.VMEM((bm, k), BF16)],
        compiler_params=_params("parallel", "arbitrary"),
        name=name,
    )(x, g.reshape(1, k), w)


def _matmul_res_kernel(*refs, n_a):
    a_refs, w_refs = refs[:n_a], refs[n_a:2 * n_a]
    res_ref, o_ref = refs[2 * n_a], refs[2 * n_a + 1]
    acc = res_ref[...]
    for a_ref, w_ref in zip(a_refs, w_refs):
        acc = acc + _dot(a_ref[...], w_ref[...])
    o_ref[...] = acc


def _matmul_res(a_list, w_list, res, name):
    m, n = res.shape
    bm = _tile(m, 512, 8)
    bn = _tile(n, 1024, LANES)
    n_a = len(a_list)
    in_specs = ([pl.BlockSpec((bm, a.shape[1]), lambda i, j: (i, 0)) for a in a_list]
                + [pl.BlockSpec((w.shape[0], bn), lambda i, j: (0, j)) for w in w_list]
                + [pl.BlockSpec((bm, bn), lambda i, j: (i, j))])
    return pl.pallas_call(
        functools.partial(_matmul_res_kernel, n_a=n_a),
        grid=(m // bm, n // bn),
        in_specs=in_specs,
        out_specs=pl.BlockSpec((bm, bn), lambda i, j: (i, j)),
        out_shape=jax.ShapeDtypeStruct((m, n), F32),
        compiler_params=_params("parallel", "arbitrary"),
        name=name,
    )(*a_list, *w_list, res)


def _swiglu_tile(xn_ref, w1_ref, w3_ref, w2_ref):
    xn = xn_ref[...]
    h1 = _dot(xn, w1_ref[...])
    h3 = _dot(xn, w3_ref[...])
    h = (h1 * jax.nn.sigmoid(h1) * h3).astype(BF16)
    return _dot(h, w2_ref[...])


def _ffn_dense_kernel(x_ref, g_ref, w1_ref, w3_ref, w2_ref, o_ref, xn_ref):
    f = pl.program_id(1)

    @pl.when(f == 0)
    def _():
        xn_ref[...] = _rms(x_ref[...], g_ref[...]).astype(BF16)

    y = _swiglu_tile(xn_ref, w1_ref, w3_ref, w2_ref)

    @pl.when(f == 0)
    def _():
        o_ref[...] = x_ref[...] + y

    @pl.when(f > 0)
    def _():
        o_ref[...] += y


def _ffn_dense(x, g, w1, w3, w2, name):
    m, d = x.shape
    ff = w1.shape[1]
    bm = _tile(m, 512, 8)
    bf = _tile(ff, 512, LANES)
    return pl.pallas_call(
        _ffn_dense_kernel,
        grid=(m // bm, ff // bf),
        in_specs=[pl.BlockSpec((bm, d), lambda i, f: (i, 0)),
                  pl.BlockSpec((1, d), lambda i, f: (0, 0)),
                  pl.BlockSpec((d, bf), lambda i, f: (0, f)),
                  pl.BlockSpec((d, bf), lambda i, f: (0, f)),
                  pl.BlockSpec((bf, d), lambda i, f: (f, 0))],
        out_specs=pl.BlockSpec((bm, d), lambda i, f: (i, 0)),
        out_shape=jax.ShapeDtypeStruct((m, d), F32),
        scratch_shapes=[pltpu.VMEM((bm, d), BF16)],
        compiler_params=_params("parallel", "arbitrary"),
        name=name,
    )(x, g.reshape(1, d), w1, w3, w2)


def _row_copy(src_hbm, row, dst, slot, sem):
    return pltpu.make_async_copy(src_hbm.at[pl.ds(row, 1), :], dst.at[pl.ds(slot, 1), :], sem)


ROW_DMA_PRIORITY = 1


def _ffn_expert_kernel(te_ref, tb_ref, tf_ref, rt_ref, x_hbm, g_ref, w1_ref, w3_ref, w2_ref, o_ref,
                       xbuf, sem, xn_ref, *, bm, n_issue):
    i = pl.program_id(0)
    f = pl.program_id(1)
    n_tiles = pl.num_programs(0)
    chunk = bm // n_issue

    def tile_rows(t, r0, n, fn):
        def body(r, carry):
            fn(_row_copy(x_hbm, rt_ref[t * bm + r0 + r], xbuf.at[t % 2], r0 + r, sem.at[t % 2]))
            return carry

        lax.fori_loop(0, n, body, 0, unroll=8)

    def occupied(t):
        return tb_ref[jnp.minimum(t, n_tiles - 1)] == t

    @pl.when(jnp.logical_and(i == 0, f == 0))
    def _():
        tile_rows(i, 0, bm, lambda cp: cp.start(priority=ROW_DMA_PRIORITY))

    @pl.when(jnp.logical_and(jnp.logical_not(occupied(i)), f == 0))
    def _():
        o_ref[...] = jnp.zeros(o_ref.shape, F32)

    @pl.when(occupied(i))
    def _():
        @pl.when(f == 0)
        def _():
            tile_rows(i, 0, bm, lambda cp: cp.wait())
            xn_ref[...] = _rms(xbuf[i % 2], g_ref[...]).astype(BF16)

        @pl.when(jnp.logical_and(f < n_issue, occupied(i + 1)))
        def _():
            tile_rows(i + 1, f * chunk, chunk, lambda cp: cp.start(priority=ROW_DMA_PRIORITY))

        y = _swiglu_tile(xn_ref, w1_ref, w3_ref, w2_ref)

        @pl.when(f == 0)
        def _():
            o_ref[...] = y

        @pl.when(f > 0)
        def _():
            o_ref[...] += y


def _ffn_experts(x, g, row_tok, w1_e, w3_e, w2_e, tile_expert, tile_block, bm, name):
    d = x.shape[1]
    n_rows = row_tok.shape[0]
    ff = w1_e.shape[2]
    bf = _tile(ff, 512, LANES)
    n_f = ff // bf
    n_tiles = n_rows // bm
    tile_f = jnp.where(tile_block == jnp.arange(n_tiles, dtype=jnp.int32), 0, 1).astype(jnp.int32)

    def f_idx(i, f, tf):
        return jnp.where(tf[i] == 0, f, n_f - 1)

    grid_spec = pltpu.PrefetchScalarGridSpec(
        num_scalar_prefetch=4,
        grid=(n_tiles, n_f),
        in_specs=[pl.BlockSpec(memory_space=pl.ANY),
                  pl.BlockSpec((1, d), lambda i, f, te, tb, tf, rt: (0, 0)),
                  pl.BlockSpec((None, d, bf), lambda i, f, te, tb, tf, rt: (te[i], 0, f_idx(i, f, tf))),
                  pl.BlockSpec((None, d, bf), lambda i, f, te, tb, tf, rt: (te[i], 0, f_idx(i, f, tf))),
                  pl.BlockSpec((None, bf, d), lambda i, f, te, tb, tf, rt: (te[i], f_idx(i, f, tf), 0))],
        out_specs=pl.BlockSpec((bm, d), lambda i, f, te, tb, tf, rt: (i, 0)),
        scratch_shapes=[pltpu.VMEM((2, bm, d), F32), pltpu.SemaphoreType.DMA((2,)), pltpu.VMEM((bm, d), BF16)],
    )
    n_issue = max(k for k in range(1, min(n_f, 8) + 1) if bm % k == 0)
    return pl.pallas_call(
        functools.partial(_ffn_expert_kernel, bm=bm, n_issue=n_issue),
        grid_spec=grid_spec,
        out_shape=jax.ShapeDtypeStruct((n_rows, d), F32),
        compiler_params=_params("arbitrary", "arbitrary"),
        name=name,
    )(tile_expert, tile_block, tile_f, row_tok, x, g.reshape(1, d), w1_e, w3_e, w2_e)


def _ln_silu(h, g, b):
    mu = jnp.mean(h, axis=-1, keepdims=True)
    hc = h - mu
    y = hc * lax.rsqrt(jnp.mean(hc * hc, axis=-1, keepdims=True) + EPS) * g + b
    return y * jax.nn.sigmoid(y)


def _conv_prompt_kernel(a_ref, gate_ref, w_ref, cb_ref, lg_ref, lb_ref, o_ref, st_ref, uext_ref, h_ref,
                        *, bt, width, halo, rc):
    t = pl.program_id(1)
    c = a_ref.shape[1]

    @pl.when(t == 0)
    def _():
        uext_ref[0:halo, :] = jnp.zeros((halo, c), F32)

    @pl.when(t > 0)
    def _():
        uext_ref[0:halo, :] = uext_ref[bt:bt + halo, :]

    uext_ref[halo:halo + bt, :] = a_ref[...] * jax.nn.sigmoid(gate_ref[...])

    off = halo - (width - 1)
    for r0 in range(0, bt, rc):
        for c0 in range(0, c, LANES):
            acc = jnp.zeros((rc, LANES), F32)
            for k in range(width):
                acc = acc + w_ref[k:k + 1, c0:c0 + LANES] * uext_ref[r0 + off + k:r0 + off + k + rc, c0:c0 + LANES]
            h_ref[r0:r0 + rc, c0:c0 + LANES] = acc
    o_ref[...] = _ln_silu(h_ref[...] + cb_ref[...], lg_ref[...], lb_ref[...]).astype(o_ref.dtype)

    @pl.when(t == pl.num_programs(1) - 1)
    def _():
        st_ref[...] = uext_ref[bt:bt + halo, :]


def _conv_prompt(h0, n_b, n_t, c, conv_w, conv_b, ln_g, ln_b):
    width = conv_w.shape[0]
    halo = -(-(width - 1) // 8) * 8
    bt = _tile(n_t, 256, 8)
    nt = n_t // bt
    out, st = pl.pallas_call(
        functools.partial(_conv_prompt_kernel, bt=bt, width=width, halo=halo, rc=_tile(bt, 64, 8)),
        grid=(n_b, nt),
        in_specs=[pl.BlockSpec((bt, c), lambda b, t: (b * nt + t, 0)),
                  pl.BlockSpec((bt, c), lambda b, t: (b * nt + t, 1)),
                  pl.BlockSpec((width, c), lambda b, t: (0, 0)),
                  pl.BlockSpec((1, c), lambda b, t: (0, 0)),
                  pl.BlockSpec((1, c), lambda b, t: (0, 0)),
                  pl.BlockSpec((1, c), lambda b, t: (0, 0))],
        out_specs=[pl.BlockSpec((bt, c), lambda b, t: (b * nt + t, 0)),
                   pl.BlockSpec((None, halo, c), lambda b, t: (b, 0, 0))],
        out_shape=[jax.ShapeDtypeStruct((n_b * n_t, c), BF16),
                   jax.ShapeDtypeStruct((n_b, halo, c), F32)],
        scratch_shapes=[pltpu.VMEM((bt + halo, c), F32), pltpu.VMEM((bt, c), F32)],
        compiler_params=_params("parallel", "arbitrary"),
        name="conv_prompt",
    )(h0, h0, conv_w, conv_b.reshape(1, c), ln_g.reshape(1, c), ln_b.reshape(1, c))
    return out, st


def _conv_sample_kernel(a_ref, gate_ref, st_ref, wst_ref, wnew_ref, cb_ref, lg_ref, lb_ref, o_ref, u_ref, h_ref,
                        *, bb, n_t):
    u = a_ref[...] * jax.nn.sigmoid(gate_ref[...])
    u_ref[...] = u
    for b in range(bb):
        past = st_ref[b]
        new = u[b * n_t:(b + 1) * n_t, :]
        for t in range(n_t):
            h_ref[b * n_t + t:b * n_t + t + 1, :] = (
                jnp.sum(past * wst_ref[t], axis=0, keepdims=True)
                + jnp.sum(new * wnew_ref[t], axis=0, keepdims=True))
    o_ref[...] = _ln_silu(h_ref[...] + cb_ref[...], lg_ref[...], lb_ref[...]).astype(o_ref.dtype)


def _conv_sample(h0, row0, n_b, n_t, c, state, conv_w, conv_b, ln_g, ln_b):
    width = conv_w.shape[0]
    n_st = width - 1
    j = jnp.arange(n_st + n_t)[None, :] - jnp.arange(n_t)[:, None]
    w_shift = jnp.where(((j >= 0) & (j < width))[..., None], conv_w[jnp.clip(j, 0, width - 1)], 0.0)
    bb = _tile(n_b, 8, 1)
    rows = bb * n_t
    blk0 = row0 // rows
    assert row0 % rows == 0 and rows % 8 == 0
    out, u = pl.pallas_call(
        functools.partial(_conv_sample_kernel, bb=bb, n_t=n_t),
        grid=(n_b // bb,),
        in_specs=[pl.BlockSpec((rows, c), lambda i: (blk0 + i, 0)),
                  pl.BlockSpec((rows, c), lambda i: (blk0 + i, 1)),
                  pl.BlockSpec((bb, n_st, c), lambda i: (i, 0, 0)),
                  pl.BlockSpec((n_t, n_st, c), lambda i: (0, 0, 0)),
                  pl.BlockSpec((n_t, n_t, c), lambda i: (0, 0, 0)),
                  pl.BlockSpec((1, c), lambda i: (0, 0)),
                  pl.BlockSpec((1, c), lambda i: (0, 0)),
                  pl.BlockSpec((1, c), lambda i: (0, 0))],
        out_specs=[pl.BlockSpec((rows, c), lambda i: (i, 0)),
                   pl.BlockSpec((rows, c), lambda i: (i, 0))],
        out_shape=[jax.ShapeDtypeStruct((n_b * n_t, c), BF16),
                   jax.ShapeDtypeStruct((n_b * n_t, c), F32)],
        scratch_shapes=[pltpu.VMEM((rows, c), F32)],
        compiler_params=_params("parallel"),
        name="conv_sample",
    )(h0, h0, state, w_shift[:, :n_st], w_shift[:, n_st:], conv_b.reshape(1, c), ln_g.reshape(1, c),
      ln_b.reshape(1, c))
    return out, u


def _rope_lanes(x, cs, sn, half):
    lane = lax.broadcasted_iota(jnp.int32, x.shape, 1)
    swapped = jnp.where(lane % (2 * half) < half, pltpu.roll(x, LANES - half, 1), pltpu.roll(x, half, 1))
    return x * cs + swapped * sn


def _mla_prep_kernel(cq_ref, ckv_ref, kr_ref, gq_ref, gkv_ref, wuq_ref, cs_ref, sn_ref,
                     q_ref, ckv_o_ref, kr_o_ref, *, n_heads, half, scale):
    cs, sn = cs_ref[...], sn_ref[...]
    q = _dot(_rms(cq_ref[...], gq_ref[...]).astype(BF16), wuq_ref[...])
    hw = 2 * LANES
    for h in range(n_heads):
        q_ref[:, h * hw:h * hw + LANES] = (q[:, h * hw:h * hw + LANES] * scale).astype(BF16)
        q_ref[:, h * hw + LANES:(h + 1) * hw] = (
            _rope_lanes(q[:, h * hw + LANES:(h + 1) * hw], cs, sn, half) * scale).astype(BF16)
    ckv_o_ref[...] = _rms(ckv_ref[...], gkv_ref[...])
    kr_o_ref[...] = _rope_lanes(kr_ref[...], cs, sn, half)


def _mla_prep(h0, col_q, q_lora, kv_lora, g_q, g_kv, wuq_pad, cos_t, sin_t, n_heads, half, scale):
    m = h0.shape[0]
    bm = _tile(m, 512, 8)
    assert col_q % q_lora == 0 and (col_q + q_lora) % kv_lora == 0 and (col_q + q_lora + kv_lora) % LANES == 0
    hw = 2 * LANES
    return pl.pallas_call(
        functools.partial(_mla_prep_kernel, n_heads=n_heads, half=half, scale=scale),
        grid=(m // bm,),
        in_specs=[pl.BlockSpec((bm, q_lora), lambda i: (i, col_q // q_lora)),
                  pl.BlockSpec((bm, kv_lora), lambda i: (i, (col_q + q_lora) // kv_lora)),
                  pl.BlockSpec((bm, LANES), lambda i: (i, (col_q + q_lora + kv_lora) // LANES)),
                  pl.BlockSpec((1, q_lora), lambda i: (0, 0)),
                  pl.BlockSpec((1, kv_lora), lambda i: (0, 0)),
                  pl.BlockSpec((q_lora, n_heads * hw), lambda i: (0, 0)),
                  pl.BlockSpec((bm, LANES), lambda i: (i, 0)),
                  pl.BlockSpec((bm, LANES), lambda i: (i, 0))],
        out_specs=[pl.BlockSpec((bm, n_heads * hw), lambda i: (i, 0)),
                   pl.BlockSpec((bm, kv_lora), lambda i: (i, 0)),
                   pl.BlockSpec((bm, LANES), lambda i: (i, 0))],
        out_shape=[jax.ShapeDtypeStruct((m, n_heads * hw), BF16),
                   jax.ShapeDtypeStruct((m, kv_lora), F32),
                   jax.ShapeDtypeStruct((m, LANES), F32)],
        compiler_params=_params("parallel"),
        name="mla_prep",
    )(h0, h0, h0, g_q.reshape(1, -1), g_kv.reshape(1, -1), wuq_pad, cos_t, sin_t)


def _mla_kv_kernel(ckv_ref, kr_ref, wuk_ref, wuv_ref, k_ref, v_ref, *, n_heads):
    c = ckv_ref[...].astype(BF16)
    kn = _dot(c, wuk_ref[...])
    v_ref[...] = _dot(c, wuv_ref[...]).astype(BF16)
    kr = kr_ref[...].astype(BF16)
    hw = 2 * LANES
    for h in range(n_heads):
        k_ref[:, h * hw:h * hw + LANES] = kn[:, h * LANES:(h + 1) * LANES].astype(BF16)
        k_ref[:, h * hw + LANES:(h + 1) * hw] = kr


def _mla_kv(ckv, kr_pad, m_p, wuk_flat, wuv_flat, n_heads):
    kv_lora = ckv.shape[1]
    bm = _tile(m_p, 512, 8)
    hw = 2 * LANES
    return pl.pallas_call(
        functools.partial(_mla_kv_kernel, n_heads=n_heads),
        grid=(m_p // bm,),
        in_specs=[pl.BlockSpec((bm, kv_lora), lambda i: (i, 0)),
                  pl.BlockSpec((bm, LANES), lambda i: (i, 0)),
                  pl.BlockSpec(wuk_flat.shape, lambda i: (0, 0)),
                  pl.BlockSpec(wuv_flat.shape, lambda i: (0, 0))],
        out_specs=[pl.BlockSpec((bm, n_heads * hw), lambda i: (i, 0)),
                   pl.BlockSpec((bm, n_heads * LANES), lambda i: (i, 0))],
        out_shape=[jax.ShapeDtypeStruct((m_p, n_heads * hw), BF16),
                   jax.ShapeDtypeStruct((m_p, n_heads * LANES), BF16)],
        compiler_params=_params("parallel"),
        name="mla_kv_expand",
    )(ckv, kr_pad, wuk_flat, wuv_flat)


def _head_matmul_kernel(a_ref, w_ref, o_ref):
    o_ref[...] = _dot(a_ref[...].astype(BF16), w_ref[...]).astype(o_ref.dtype)


def _head_matmul(a, row0, m, col0, col_stride, w, out_dtype, name):
    n_heads, k, n = w.shape
    assert col0 % k == 0 and col_stride % k == 0 and row0 % m == 0
    return pl.pallas_call(
        _head_matmul_kernel,
        grid=(n_heads,),
        in_specs=[pl.BlockSpec((m, k), lambda h: (row0 // m, (col0 + h * col_stride) // k)),
                  pl.BlockSpec((None, k, n), lambda h: (h, 0, 0))],
        out_specs=pl.BlockSpec((m, n), lambda h: (0, h)),
        out_shape=jax.ShapeDtypeStruct((m, n_heads * n), out_dtype),
        compiler_params=_params("parallel"),
        name=name,
    )(a, w)


def _mla_flash_kernel(q_ref, k_ref, v_ref, o_ref, m_ref, l_ref, acc_ref, *, blk, hp):
    qi = pl.program_id(2)
    hw = 2 * LANES
    m_ref[...] = jnp.full(m_ref.shape, -jnp.inf, F32)
    l_ref[...] = jnp.zeros(l_ref.shape, F32)
    acc_ref[...] = jnp.zeros(acc_ref.shape, F32)

    def step(kb, masked):
        r0 = pl.multiple_of(kb * blk, blk)
        for j in range(hp):
            s = _dot_nt(q_ref[:, j * hw:(j + 1) * hw], k_ref[pl.ds(r0, blk), j * hw:(j + 1) * hw])
            if masked:
                row = lax.broadcasted_iota(jnp.int32, s.shape, 0)
                col = lax.broadcasted_iota(jnp.int32, s.shape, 1)
                s = jnp.where(col <= row, s, -jnp.inf)
            m_old = m_ref[j]
            m_new = jnp.maximum(m_old, jnp.max(s, axis=-1, keepdims=True))
            p = jnp.exp(s - m_new)
            corr = jnp.exp(m_old - m_new)
            l_ref[j] = l_ref[j] * corr + jnp.sum(p, axis=-1, keepdims=True)
            acc_ref[j] = acc_ref[j] * corr + _dot(p.astype(BF16), v_ref[pl.ds(r0, blk), j * LANES:(j + 1) * LANES])
            m_ref[j] = m_new

    step(qi, True)

    def body(kb, carry):
        step(kb, False)
        return carry

    lax.fori_loop(0, qi, body, 0)
    for j in range(hp):
        o_ref[:, j * LANES:(j + 1) * LANES] = (acc_ref[j] / l_ref[j]).astype(o_ref.dtype)


def _mla_flash(q_pad, k_pad, v, n_b, n_t, n_heads):
    blk = _tile(n_t, 512, 8)
    nq = n_t // blk
    hp = _tile(n_heads, 2, 1)
    hw = 2 * LANES
    return pl.pallas_call(
        functools.partial(_mla_flash_kernel, blk=blk, hp=hp),
        grid=(n_b, n_heads // hp, nq),
        in_specs=[pl.BlockSpec((blk, hp * hw), lambda b, h, i: (b * nq + i, h)),
                  pl.BlockSpec((n_t, hp * hw), lambda b, h, i: (b, h)),
                  pl.BlockSpec((n_t, hp * LANES), lambda b, h, i: (b, h))],
        out_specs=pl.BlockSpec((blk, hp * LANES), lambda b, h, i: (b * nq + i, h)),
        out_shape=jax.ShapeDtypeStruct((n_b * n_t, n_heads * LANES), BF16),
        scratch_shapes=[pltpu.VMEM((hp, blk, 1), F32), pltpu.VMEM((hp, blk, 1), F32),
                        pltpu.VMEM((hp, blk, LANES), F32)],
        compiler_params=_params("parallel", "parallel", "arbitrary"),
        name="mla_prompt_attn",
    )(q_pad, k_pad, v)


def _first_slot(i):
    return i % 2


def _loop_slot(s):
    return 2 + (s - 1) % 2


def _mla_decode_kernel(pt_ref, ql_ref, qr_ref, cnew_ref, rnew_ref, ckv_hbm, kr_hbm, o_ref,
                       cbuf, rbuf, csem, rsem, m_ref, l_ref, acc_ref, *, n_seq, n_pg, n_steps, n_t, n_heads):
    i = pl.program_id(0)

    def copies(ii, step, slot):
        out = []
        for q in range(n_seq):
            for g in range(n_pg):
                pg = pt_ref[ii * n_seq + q, step * n_pg + g]
                out.append(pltpu.make_async_copy(ckv_hbm.at[pg], cbuf.at[slot, q * n_pg + g], csem.at[slot]))
                out.append(pltpu.make_async_copy(kr_hbm.at[pg], rbuf.at[slot, q * n_pg + g], rsem.at[slot]))
        return out

    def start(ii, step, slot):
        for cp in copies(ii, step, slot):
            cp.start()

    def wait(ii, step, slot):
        for cp in copies(ii, step, slot):
            cp.wait()

    @pl.when(i == 0)
    def _():
        start(i, 0, _first_slot(i))

    @pl.when(i + 1 < pl.num_programs(0))
    def _():
        start(i + 1, 0, _first_slot(i + 1))

    if n_steps > 1:
        start(i, 1, _loop_slot(1))

    rows = ql_ref.shape[1]
    t_row = lax.broadcasted_iota(jnp.int32, (rows, 1), 0) // n_heads
    for q in range(n_seq):
        qlf, qrf = ql_ref[q].astype(F32), qr_ref[q].astype(F32)
        m = jnp.full((rows, 1), -jnp.inf, F32)
        l = jnp.zeros((rows, 1), F32)
        acc = jnp.zeros(acc_ref.shape[1:], F32)
        for t in range(n_t):
            c = cnew_ref[q, t:t + 1, :]
            s = (jnp.sum(qlf * c, axis=-1, keepdims=True)
                 + jnp.sum(qrf * rnew_ref[q, t:t + 1, :], axis=-1, keepdims=True))
            s = jnp.where(t <= t_row, s, -jnp.inf)
            m_new = jnp.maximum(m, s)
            corr = jnp.exp(m - m_new)
            p = jnp.exp(s - m_new)
            l = l * corr + p
            acc = acc * corr + p * c
            m = m_new
        m_ref[q] = m
        l_ref[q] = l
        acc_ref[q] = acc

    def compute(slot):
        rope = rbuf.shape[2]
        for q in range(n_seq):
            ql = ql_ref[q]
            qr = qr_ref[q][:, :rope]
            ckv = [cbuf[slot, q * n_pg + g].astype(BF16) for g in range(n_pg)]
            s = jnp.concatenate(
                [_dot_nt(ql, c) + _dot(qr, rbuf[slot, q * n_pg + g].astype(BF16)) for g, c in enumerate(ckv)],
                axis=-1)
            m_old = m_ref[q]
            m_new = jnp.maximum(m_old, jnp.max(s, axis=-1, keepdims=True))
            p = jnp.exp(s - m_new)
            corr = jnp.exp(m_old - m_new)
            l_ref[q] = l_ref[q] * corr + jnp.sum(p, axis=-1, keepdims=True)
            acc = acc_ref[q] * corr
            pg = ckv[0].shape[0]
            for g, c in enumerate(ckv):
                acc = acc + _dot(p[:, g * pg:(g + 1) * pg].astype(BF16), c)
            acc_ref[q] = acc
            m_ref[q] = m_new

    wait(i, 0, _first_slot(i))
    compute(_first_slot(i))

    def body(s, carry):
        @pl.when(s + 1 < n_steps)
        def _():
            start(i, s + 1, _loop_slot(s + 1))

        wait(i, s, _loop_slot(s))
        compute(_loop_slot(s))
        return carry

    lax.fori_loop(1, n_steps, body, 0)
    for q in range(n_seq):
        o_ref[q] = acc_ref[q] / l_ref[q]


def _mla_decode(page_table, q_lat, q_rope, ckv_new, kr_new, cache_ckv, cache_kr_t, n_heads):
    n_b, rows, c = q_lat.shape
    n_t = ckv_new.shape[1]
    n_pages = page_table.shape[1]
    page = cache_ckv.shape[1]
    rope = cache_kr_t.shape[1]
    n_pg = _tile(n_pages, 8, 1)
    n_seq = _tile(n_b, 2, 1)
    grid_spec = pltpu.PrefetchScalarGridSpec(
        num_scalar_prefetch=1,
        grid=(n_b // n_seq,),
        in_specs=[pl.BlockSpec((n_seq, rows, c), lambda i, pt: (i, 0, 0)),
                  pl.BlockSpec((n_seq, rows, LANES), lambda i, pt: (i, 0, 0)),
                  pl.BlockSpec((n_seq, n_t, c), lambda i, pt: (i, 0, 0)),
                  pl.BlockSpec((n_seq, n_t, LANES), lambda i, pt: (i, 0, 0)),
                  pl.BlockSpec(memory_space=pl.ANY),
                  pl.BlockSpec(memory_space=pl.ANY)],
        out_specs=pl.BlockSpec((n_seq, rows, c), lambda i, pt: (i, 0, 0)),
        scratch_shapes=[pltpu.VMEM((4, n_seq * n_pg, page, c), F32), pltpu.VMEM((4, n_seq * n_pg, rope, page), F32),
                        pltpu.SemaphoreType.DMA((4,)), pltpu.SemaphoreType.DMA((4,)),
                        pltpu.VMEM((n_seq, rows, 1), F32), pltpu.VMEM((n_seq, rows, 1), F32),
                        pltpu.VMEM((n_seq, rows, c), F32)],
    )
    return pl.pallas_call(
        functools.partial(_mla_decode_kernel, n_seq=n_seq, n_pg=n_pg, n_steps=n_pages // n_pg, n_t=n_t,
                          n_heads=n_heads),
        grid_spec=grid_spec,
        out_shape=jax.ShapeDtypeStruct((n_b, rows, c), F32),
        compiler_params=_params("arbitrary"),
        name="mla_sample_attn",
    )(page_table, q_lat, q_rope, ckv_new, kr_new, cache_ckv, cache_kr_t)


SB_DEAD_TAIL = -104.0


def _sb_block(q, k, v, u_ref, tail, mask):
    z = _dot_nt(q, k)
    log_beta = jnp.minimum(z, 0.0) - jnp.log(1.0 + jnp.exp(-jnp.abs(z)))
    log_1m = log_beta - z
    if mask is not None:
        log_1m = jnp.where(mask, log_1m, 0.0)
    hi = lax.bitcast_convert_type(lax.bitcast_convert_type(log_1m, jnp.uint32) & jnp.uint32(0xFFFF0000), F32)
    later = _dot(hi.astype(BF16), u_ref[...]) + _dot((log_1m - hi).astype(BF16), u_ref[...]) + tail
    w = jnp.exp(log_beta + later)
    if mask is not None:
        w = jnp.where(mask, w, 0.0)
    return _dot(w.astype(BF16), v), tail + jnp.sum(log_1m, axis=-1, keepdims=True)


def _suffix_matrix(n):
    j = jnp.arange(n)
    return (j[:, None] > j[None, :]).astype(BF16)


def _alive(tail_ref):
    return (jnp.max(tail_ref[...]) > SB_DEAD_TAIL).astype(jnp.int32)


def _sb_prompt_kernel(q_ref, k_ref, v_ref, u_ref, o_ref, acc_ref, tail_ref, *, blk, hp, dh, scale):
    qi = pl.program_id(3)

    def q_head(j):
        return (q_ref[:, j * dh:(j + 1) * dh] * scale).astype(BF16)

    def kv(kb):
        r0 = pl.multiple_of(kb * blk, blk)
        return k_ref[pl.ds(r0, blk), :].astype(BF16), v_ref[pl.ds(r0, blk), :].astype(BF16)

    row = lax.broadcasted_iota(jnp.int32, (blk, blk), 0)
    col = lax.broadcasted_iota(jnp.int32, (blk, blk), 1)
    k, v = kv(qi)
    for j in range(hp):
        acc, tail = _sb_block(q_head(j), k, v, u_ref, jnp.zeros((blk, 1), F32), col < row)
        acc_ref[j] = acc
        tail_ref[j] = tail

    def cond(c):
        return jnp.logical_and(c[0] < qi, c[1] == 1)

    def body(c):
        k, v = kv(qi - 1 - c[0])
        for j in range(hp):
            acc, tail = _sb_block(q_head(j), k, v, u_ref, tail_ref[j], None)
            acc_ref[j] += acc
            tail_ref[j] = tail
        return c[0] + 1, _alive(tail_ref)

    lax.while_loop(cond, body, (jnp.int32(0), _alive(tail_ref)))
    for j in range(hp):
        o_ref[:, j * dh:(j + 1) * dh] = acc_ref[j].astype(o_ref.dtype)


def _sb_prompt(h1, n_b, n_t, n_heads, n_kv, dh, scale):
    blk = _tile(n_t, 256, 8)
    nq = n_t // blk
    group = n_heads // n_kv
    hp = _tile(group, 4, 1)
    gp = group // hp
    return pl.pallas_call(
        functools.partial(_sb_prompt_kernel, blk=blk, hp=hp, dh=dh, scale=scale),
        grid=(n_b, n_kv, gp, nq),
        in_specs=[pl.BlockSpec((blk, hp * dh), lambda b, g, r, i: (b * nq + i, g * gp + r)),
                  pl.BlockSpec((n_t, dh), lambda b, g, r, i: (b, n_heads + g)),
                  pl.BlockSpec((n_t, dh), lambda b, g, r, i: (b, n_heads + n_kv + g)),
                  pl.BlockSpec((blk, blk), lambda b, g, r, i: (0, 0))],
        out_specs=pl.BlockSpec((blk, hp * dh), lambda b, g, r, i: (b * nq + i, g * gp + r)),
        out_shape=jax.ShapeDtypeStruct((n_b * n_t, n_heads * dh), BF16),
        scratch_shapes=[pltpu.VMEM((hp, blk, dh), F32), pltpu.VMEM((hp, blk, 1), F32)],
        compiler_params=_params("parallel", "parallel", "parallel", "arbitrary"),
        name="sb_prompt_attn",
    )(h1, h1, h1, _suffix_matrix(blk))


def _sb_decode_kernel(pt_ref, q_ref, kn_ref, vn_ref, u_ref, k_hbm, v_hbm, o_ref,
                      kbuf, vbuf, ksem, vsem, acc_ref, tail_ref, alive_ref,
                      *, n_pg, n_steps, n_pages, page, n_t, n_kv, group, dh, scale):
    b = pl.program_id(0)

    def copies(bb, step, slot):
        out = []
        for g in range(n_pg):
            pg = pt_ref[bb, n_pages - 1 - (step * n_pg + g)]
            out.append(pltpu.make_async_copy(k_hbm.at[pg], kbuf.at[slot, g], ksem.at[slot]))
            out.append(pltpu.make_async_copy(v_hbm.at[pg], vbuf.at[slot, g], vsem.at[slot]))
        return out

    def start(bb, step, slot):
        for cp in copies(bb, step, slot):
            cp.start()

    def wait(bb, step, slot):
        for cp in copies(bb, step, slot):
            cp.wait()

    @pl.when(b == 0)
    def _():
        start(b, 0, _first_slot(b))

    @pl.when(b + 1 < pl.num_programs(0))
    def _():
        start(b + 1, 0, _first_slot(b + 1))

    if n_steps > 1:
        start(b, 1, _loop_slot(1))

    t_row = lax.broadcasted_iota(jnp.int32, (n_t * group, 1), 0) // group
    for g in range(n_kv):
        q = (q_ref[g] * scale).astype(BF16).astype(F32)
        tail = jnp.zeros((n_t * group, 1), F32)
        acc = jnp.zeros((n_t * group, dh), F32)
        for t in reversed(range(n_t)):
            z = jnp.sum(q * kn_ref[t:t + 1, g * dh:(g + 1) * dh], axis=-1, keepdims=True)
            sp = jnp.log1p(jnp.exp(-jnp.abs(z)))
            mask = t < t_row
            w = jnp.where(mask, jnp.exp(jnp.minimum(z, 0.0) - sp + tail), 0.0)
            acc = acc + w * vn_ref[t:t + 1, g * dh:(g + 1) * dh]
            tail = tail + jnp.where(mask, jnp.minimum(-z, 0.0) - sp, 0.0)
        acc_ref[g] = acc
        tail_ref[g] = tail
    alive_ref[0] = 1

    def compute(slot):
        for g in range(n_pg):
            @pl.when(alive_ref[0] == 1)
            def _():
                for grp in range(n_kv):
                    q = (q_ref[grp] * scale).astype(BF16)
                    d, tail = _sb_block(q, kbuf[slot, g, pl.ds(grp, page, stride=n_kv), :].astype(BF16),
                                        vbuf[slot, g, pl.ds(grp, page, stride=n_kv), :].astype(BF16),
                                        u_ref, tail_ref[grp], None)
                    acc_ref[grp] += d
                    tail_ref[grp] = tail
                alive_ref[0] = _alive(tail_ref)

    wait(b, 0, _first_slot(b))
    compute(_first_slot(b))

    if n_steps > 1:
        def cond(c):
            return jnp.logical_and(c[0] < n_steps, c[1] == 1)

        def body(c):
            s = c[0]

            @pl.when(s + 1 < n_steps)
            def _():
                start(b, s + 1, _loop_slot(s + 1))

            wait(b, s, _loop_slot(s))
            compute(_loop_slot(s))
            return s + 1, alive_ref[0]

        s_end, _ = lax.while_loop(cond, body, (jnp.int32(1), alive_ref[0]))

        @pl.when(s_end < n_steps)
        def _():
            wait(b, s_end, _loop_slot(s_end))

    o_ref[...] = acc_ref[...]


def _sb_decode(page_table, q, k_new, v_new, cache_k, cache_v, scale):
    n_b, n_kv, rows, dh = q.shape
    n_t = k_new.shape[1]
    group = rows // n_t
    n_pages = page_table.shape[1]
    page = cache_k.shape[1] // n_kv
    n_pg = _tile(n_pages, 4, 1)
    grid_spec = pltpu.PrefetchScalarGridSpec(
        num_scalar_prefetch=1,
        grid=(n_b,),
        in_specs=[pl.BlockSpec((None, n_kv, rows, dh), lambda b, pt: (b, 0, 0, 0)),
                  pl.BlockSpec((None, n_t, n_kv * dh), lambda b, pt: (b, 0, 0)),
                  pl.BlockSpec((None, n_t, n_kv * dh), lambda b, pt: (b, 0, 0)),
                  pl.BlockSpec((page, page), lambda b, pt: (0, 0)),
                  pl.BlockSpec(memory_space=pl.ANY),
                  pl.BlockSpec(memory_space=pl.ANY)],
        out_specs=pl.BlockSpec((None, n_kv, rows, dh), lambda b, pt: (b, 0, 0, 0)),
        scratch_shapes=[pltpu.VMEM((4, n_pg, page * n_kv, dh), F32), pltpu.VMEM((4, n_pg, page * n_kv, dh), F32),
                        pltpu.SemaphoreType.DMA((4,)), pltpu.SemaphoreType.DMA((4,)),
                        pltpu.VMEM((n_kv, rows, dh), F32), pltpu.VMEM((n_kv, rows, 1), F32),
                        pltpu.SMEM((1,), jnp.int32)],
    )
    return pl.pallas_call(
        functools.partial(_sb_decode_kernel, n_pg=n_pg, n_steps=n_pages // n_pg, n_pages=n_pages, page=page,
                          n_t=n_t, n_kv=n_kv, group=group, dh=dh, scale=scale),
        grid_spec=grid_spec,
        out_shape=jax.ShapeDtypeStruct((n_b, n_kv, rows, dh), F32),
        compiler_params=_params("arbitrary"),
        name="sb_sample_attn",
    )(page_table, q, k_new, v_new, _suffix_matrix(page), cache_k, cache_v)


def _router_kernel(x_ref, g_ref, wr_ref, gate_ref, sel_ref, *, n_exp):
    xn = _rms(x_ref[...], g_ref[...])
    logits = jnp.concatenate(
        [jnp.sum(xn * wr_ref[e:e + 1, :], axis=-1, keepdims=True) for e in range(n_exp)], axis=-1)
    e_id = lax.broadcasted_iota(jnp.int32, logits.shape, 1)
    m1 = jnp.max(logits, axis=-1, keepdims=True)
    i1 = jnp.min(jnp.where(logits == m1, e_id, n_exp), axis=-1, keepdims=True)
    rest = jnp.where(e_id == i1, -jnp.inf, logits)
    m2 = jnp.max(rest, axis=-1, keepdims=True)
    i2 = jnp.min(jnp.where(rest == m2, e_id, n_exp), axis=-1, keepdims=True)
    e2 = jnp.exp(m2 - m1)
    den = 1.0 + e2
    gate_ref[...] = jnp.where(e_id == i1, 1.0 / den, 0.0) + jnp.where(e_id == i2, e2 / den, 0.0)
    sel_ref[...] = jnp.where((e_id == i1) | (e_id == i2), 1.0, 0.0)


def _router(x, g, w_router):
    m, d = x.shape
    n_exp = w_router.shape[1]
    bm = _tile(m, 512, 8)
    return pl.pallas_call(
        functools.partial(_router_kernel, n_exp=n_exp),
        grid=(m // bm,),
        in_specs=[pl.BlockSpec((bm, d), lambda i: (i, 0)),
                  pl.BlockSpec((1, d), lambda i: (0, 0)),
                  pl.BlockSpec((n_exp, d), lambda i: (0, 0))],
        out_specs=[pl.BlockSpec((bm, n_exp), lambda i: (i, 0))] * 2,
        out_shape=[jax.ShapeDtypeStruct((m, n_exp), F32)] * 2,
        compiler_params=_params("parallel"),
        name="moe_router",
    )(x, g.reshape(1, d), w_router.T)


def _row_index_kernel(p0_ref, p1_ref, o_ref):
    def zero(r, carry):
        o_ref[r] = 0
        return carry

    def put(t, carry):
        o_ref[p0_ref[t]] = t
        o_ref[p1_ref[t]] = t
        return carry

    lax.fori_loop(0, o_ref.shape[0], zero, 0, unroll=16)
    lax.fori_loop(0, p0_ref.shape[0], put, 0, unroll=8)


def _row_index(pos0, pos1, n_rows):
    return pl.pallas_call(
        _row_index_kernel,
        in_specs=[pl.BlockSpec(memory_space=pltpu.SMEM)] * 2,
        out_specs=pl.BlockSpec(memory_space=pltpu.SMEM),
        out_shape=jax.ShapeDtypeStruct((n_rows,), jnp.int32),
        name="moe_row_index",
    )(pos0, pos1)


def _combine_kernel(p0_ref, p1_ref, x_ref, g_ref, g0_ref, g1_ref, y_hbm, o_ref, buf_ref, sem, *, bm):
    i = pl.program_id(0)

    def tile_rows(t, fn):
        def body(r, carry):
            fn(_row_copy(y_hbm, p0_ref[t * bm + r], buf_ref.at[t % 2, 0], r, sem.at[t % 2]), 0)
            fn(_row_copy(y_hbm, p1_ref[t * bm + r], buf_ref.at[t % 2, 1], r, sem.at[t % 2]), 1)
            return carry

        lax.fori_loop(0, bm, body, 0, unroll=8)

    @pl.when(i == 0)
    def _():
        tile_rows(i, lambda cp, thread: cp.start(priority=thread))

    @pl.when(i + 1 < pl.num_programs(0))
    def _():
        tile_rows(i + 1, lambda cp, thread: cp.start(priority=thread))

    tile_rows(i, lambda cp, thread: cp.wait())
    y = g0_ref[...] * buf_ref[i % 2, 0] + g1_ref[...] * buf_ref[i % 2, 1]
    o_ref[...] = _rms(x_ref[...] + y, g_ref[...])


def _combine_norm(x, g, ys, pos0, pos1, g0, g1, bm):
    m, d = x.shape
    grid_spec = pltpu.PrefetchScalarGridSpec(
        num_scalar_prefetch=2,
        grid=(m // bm,),
        in_specs=[pl.BlockSpec((bm, d), lambda i, p0, p1: (i, 0)),
                  pl.BlockSpec((1, d), lambda i, p0, p1: (0, 0)),
                  pl.BlockSpec((bm, 1), lambda i, p0, p1: (i, 0)),
                  pl.BlockSpec((bm, 1), lambda i, p0, p1: (i, 0)),
                  pl.BlockSpec(memory_space=pl.ANY)],
        out_specs=pl.BlockSpec((bm, d), lambda i, p0, p1: (i, 0)),
        scratch_shapes=[pltpu.VMEM((2, 2, bm, d), F32), pltpu.SemaphoreType.DMA((2,))],
    )
    return pl.pallas_call(
        functools.partial(_combine_kernel, bm=bm),
        grid_spec=grid_spec,
        out_shape=jax.ShapeDtypeStruct((m, d), F32),
        compiler_params=_params("arbitrary"),
        name="moe_combine_norm",
    )(pos0, pos1, x, g.reshape(1, d), g0, g1, ys)


def _moe_final(x, g_ffn, g_final, w_router, w1_e, w3_e, w2_e):
    m, d = x.shape
    n_exp = w_router.shape[1]
    bm = _tile(m, 512, 8)
    gate, sel = _router(x, g_ffn, w_router)
    sel = sel > 0.0
    cnt = jnp.cumsum(sel.astype(jnp.int32), axis=0)
    n_e = cnt[-1]
    padded = (n_e + bm - 1) // bm * bm
    ends = jnp.cumsum(padded)
    dest = (ends - padded)[None, :] + cnt - 1
    n_rows = (TOP_K * m + n_exp * (bm - 1)) // bm * bm
    n_tiles = n_rows // bm
    n_used = jnp.maximum(ends[-1] // bm, 1)
    tile_block = jnp.minimum(jnp.arange(n_tiles, dtype=jnp.int32), n_used - 1)
    tile_expert = jnp.minimum(jnp.sum(tile_block[:, None] * bm >= ends[None, :], axis=1), n_exp - 1)
    first = jnp.argmax(sel, axis=1)[:, None]
    last = (n_exp - 1 - jnp.argmax(sel[:, ::-1], axis=1))[:, None]
    pos0 = jnp.take_along_axis(dest, first, axis=1)[:, 0].astype(jnp.int32)
    pos1 = jnp.take_along_axis(dest, last, axis=1)[:, 0].astype(jnp.int32)
    g0 = jnp.take_along_axis(gate, first, axis=1)
    g1 = jnp.take_along_axis(gate, last, axis=1)

    row_tok = _row_index(pos0, pos1, n_rows)
    ys = _ffn_experts(x, g_ffn, row_tok, w1_e, w3_e, w2_e, tile_expert.astype(jnp.int32), tile_block, bm,
                      "moe_experts")
    return _combine_norm(x, g_final, ys, pos0, pos1, g0, g1, _tile(m, 256, 8))


def kernel(x_prompt, x_sample, state_conv0, cache_mla_ckv, cache_mla_krope, cache_sb_k, cache_sb_v, page_table,
           ln_mix0, w_in0, conv_w, conv_b, conv_ln_g, conv_ln_b, q_norm_g, kv_norm_g, w_uq, w_uk, w_uv, w_out0,
           ln_ffn0, w1_d, w3_d, w2_d, ln_mix1, w_in1, w_out1, ln_ffn1, w_router, w1_e, w3_e, w2_e, ln_final):
    n_bp, n_tp, d = x_prompt.shape
    n_bs, n_ts, _ = x_sample.shape
    m_p, m_s = n_bp * n_tp, n_bs * n_ts
    c_conv = conv_w.shape[1]
    q_lora, n_hm, qk_dim = w_uq.shape
    kv_lora, _, nope = w_uk.shape
    v_dim = w_uv.shape[2]
    rope = qk_dim - nope
    half = rope // 2
    page = cache_mla_ckv.shape[1]
    past_len = page_table.shape[1] * page
    n_kv, dh = cache_sb_k.shape[2], cache_sb_k.shape[3]
    n_hs = w_out1.shape[0] // dh
    group = n_hs // n_kv
    assert nope == LANES and v_dim == LANES and dh == LANES and rope <= LANES

    x = jnp.concatenate([x_prompt.reshape(m_p, d), x_sample.reshape(m_s, d)], axis=0)

    d_in0 = w_in0.shape[1]
    d_in0_pad = -(-d_in0 // 640) * 640 if d_in0 > 640 else -(-d_in0 // LANES) * LANES
    w_in0_p = jnp.pad(w_in0, ((0, 0), (0, d_in0_pad - d_in0))).astype(BF16)
    hw = 2 * LANES
    wuq_pad = jnp.pad(w_uq, ((0, 0), (0, 0), (0, hw - qk_dim))).reshape(q_lora, n_hm * hw).astype(BF16)
    wuk_flat = w_uk.reshape(kv_lora, n_hm * nope).astype(BF16)
    wuv_flat = w_uv.reshape(kv_lora, n_hm * v_dim).astype(BF16)
    wuk_t = jnp.transpose(w_uk, (1, 2, 0)).astype(BF16)
    wuv_h = jnp.transpose(w_uv, (1, 0, 2)).astype(BF16)
    w_out0_b = w_out0.astype(BF16)
    w1_db, w3_db, w2_db = w1_d.astype(BF16), w3_d.astype(BF16), w2_d.astype(BF16)
    w_in1_b, w_out1_b = w_in1.astype(BF16), w_out1.astype(BF16)
    w1_eb, w3_eb, w2_eb = w1_e.astype(BF16), w3_e.astype(BF16), w2_e.astype(BF16)

    pos = jnp.concatenate([jnp.tile(jnp.arange(n_tp, dtype=F32), n_bp),
                           jnp.tile(past_len + jnp.arange(n_ts, dtype=F32), n_bs)])
    freqs = ROPE_THETA ** (-jnp.arange(half, dtype=F32) / half)
    ang = pos[:, None] * freqs[None, :]
    zpad = jnp.zeros((m_p + m_s, LANES - rope), F32)
    cos_t = jnp.concatenate([jnp.cos(ang), jnp.cos(ang), zpad], axis=1)
    sin_t = jnp.concatenate([-jnp.sin(ang), jnp.sin(ang), zpad], axis=1)

    h0 = _norm_matmul(x, ln_mix0, w_in0_p, "in_proj0")
    conv_p, st_p = _conv_prompt(h0, n_bp, n_tp, c_conv, conv_w, conv_b, conv_ln_g, conv_ln_b)
    conv_s, u_s = _conv_sample(h0, m_p, n_bs, n_ts, c_conv, state_conv0, conv_w, conv_b, conv_ln_g, conv_ln_b)
    n_state = conv_w.shape[0] - 1
    conv_state_p = st_p[:, st_p.shape[1] - n_state:, :]
    conv_state_s = jnp.concatenate([state_conv0, u_s.reshape(n_bs, n_ts, c_conv)], axis=1)[:, -n_state:, :]

    mla_scale = float(qk_dim) ** -0.5
    q_pad, ckv, kr_pad = _mla_prep(h0, 2 * c_conv, q_lora, kv_lora, q_norm_g, kv_norm_g, wuq_pad, cos_t, sin_t,
                                   n_hm, half, mla_scale)
    k_pad, v_p = _mla_kv(ckv, kr_pad, m_p, wuk_flat, wuv_flat, n_hm)
    o_p = _mla_flash(q_pad, k_pad, v_p, n_bp, n_tp, n_hm)

    q_lat = _head_matmul(q_pad, m_p, m_s, 0, hw, wuk_t, BF16, "mla_q_absorb")
    q_lat = q_lat.reshape(n_bs, n_ts * n_hm, kv_lora)
    q_rope_s = q_pad[m_p:].reshape(n_bs, n_ts * n_hm, hw)[:, :, LANES:]
    o_lat = _mla_decode(page_table, q_lat, q_rope_s, ckv[m_p:].reshape(n_bs, n_ts, kv_lora),
                        kr_pad[m_p:].reshape(n_bs, n_ts, LANES), cache_mla_ckv,
                        jnp.transpose(cache_mla_krope, (0, 2, 1)), n_hm)
    o_s = _head_matmul(o_lat.reshape(m_s, n_hm * kv_lora), 0, m_s, 0, kv_lora, wuv_h, BF16, "mla_v_absorb")

    conv_out = jnp.concatenate([conv_p, conv_s], axis=0)
    mla_out = jnp.concatenate([o_p, o_s], axis=0)
    x = _matmul_res([conv_out, mla_out], [w_out0_b[:c_conv], w_out0_b[c_conv:]], x, "out_proj0")
    x = _ffn_dense(x, ln_ffn0, w1_db, w3_db, w2_db, "ffn0")

    h1 = _norm_matmul(x, ln_mix1, w_in1_b, "in_proj1")
    sb_scale = float(dh) ** -0.5
    so_p = _sb_prompt(h1, n_bp, n_tp, n_hs, n_kv, dh, sb_scale)
    d_q = n_hs * dh
    h1_s = h1[m_p:]
    q_s = jnp.transpose(h1_s[:, :d_q].reshape(n_bs, n_ts, n_kv, group, dh), (0, 2, 1, 3, 4))
    q_s = q_s.reshape(n_bs, n_kv, n_ts * group, dh)
    k_s = h1_s[:, d_q:d_q + n_kv * dh].reshape(n_bs, n_ts, n_kv * dh)
    v_s = h1_s[:, d_q + n_kv * dh:].reshape(n_bs, n_ts, n_kv * dh)
    so_s = _sb_decode(page_table, q_s, k_s, v_s, cache_sb_k.reshape(-1, page * n_kv, dh),
                      cache_sb_v.reshape(-1, page * n_kv, dh), sb_scale)
    so_s = jnp.transpose(so_s.reshape(n_bs, n_kv, n_ts, group, dh), (0, 2, 1, 3, 4)).reshape(m_s, d_q)
    sb_out = jnp.concatenate([so_p, so_s.astype(BF16)], axis=0)
    x = _matmul_res([sb_out], [w_out1_b], x, "out_proj1")

    y = _moe_final(x, ln_ffn1, ln_final, w_router, w1_eb, w3_eb, w2_eb)

    return (y[:m_p].reshape(n_bp, n_tp, d), y[m_p:].reshape(n_bs, n_ts, d),
            conv_state_p, ckv[:m_p].reshape(n_bp, n_tp, kv_lora), kr_pad[:m_p, :rope].reshape(n_bp, n_tp, rope),
            h1[:m_p, d_q:d_q + n_kv * dh].reshape(n_bp, n_tp, n_kv, dh),
            h1[:m_p, d_q + n_kv * dh:].reshape(n_bp, n_tp, n_kv, dh),
            conv_state_s, ckv[m_p:].reshape(n_bs, n_ts, kv_lora), kr_pad[m_p:, :rope].reshape(n_bs, n_ts, rope),
            k_s.reshape(n_bs, n_ts, n_kv, dh), v_s.reshape(n_bs, n_ts, n_kv, dh))
```
